```python
import math
import jax, jax.numpy as jnp
from jax import lax
import numpy as np


D_MODEL = 4096
BATCH = 1
SEQ = 16384
DEPTH = 1
DEC_BATCH = 1
DEC_SEQ = 8192
PAST_LEN = 128

D_MIX = D_MODEL
HEAD_DIM = 128
ATTN_W = D_MIX // 2
N_Q_HEADS = ATTN_W // HEAD_DIM
N_KV_HEADS = 4
GROUP = N_Q_HEADS // N_KV_HEADS
KV_W = N_KV_HEADS * HEAD_DIM
WINDOW = 128
BLOCK = 128
N_BUCKETS = 32
MAX_DISTANCE = 128
LRU_W = D_MIX - ATTN_W
LRU_BLOCKS = 16
LRU_BW = LRU_W // LRU_BLOCKS
CONV_W = 4
LRU_C = 8.0
PROJ_W = ATTN_W + 2 * KV_W + 2 * LRU_W
N_EXPERTS = 32
TOP_K = 4
D_FF = D_MODEL
SWIGLU_LIMIT = 7.0
SWIGLU_ALPHA = 1.702
MOE_BLOCK = 512
N_MOD = 6
DEEPNORM_ALPHA = (2 * DEPTH) ** 0.25
DEEPNORM_BETA = (8 * DEPTH) ** -0.25
EPS = 1e-5

kernel_name = 'hymba_swa_rglru_moe_encoder'


def layer_norm(x, g, b):
    xf = x.astype(jnp.float32)
    mu = jnp.mean(xf, axis=-1, keepdims=True)
    var = jnp.mean(jnp.square(xf - mu), axis=-1, keepdims=True)
    y = (xf - mu) * lax.rsqrt(var + EPS)
    return (y * g.astype(jnp.float32) + b.astype(jnp.float32)).astype(x.dtype)


def rms_norm(x, g):
    xf = x.astype(jnp.float32)
    y = xf * lax.rsqrt(jnp.mean(jnp.square(xf), axis=-1, keepdims=True) + EPS)
    return (y * g.astype(jnp.float32)).astype(x.dtype)


def t5_relative_bucket(rel):
    half = N_BUCKETS // 2
    max_exact = half // 2
    ret = jnp.where(rel > 0, half, 0)
    n = jnp.abs(rel)
    nf = jnp.maximum(n, 1).astype(jnp.float32)
    large = max_exact + (jnp.log(nf / max_exact) / math.log(MAX_DISTANCE / max_exact)
                         * (half - max_exact)).astype(jnp.int32)
    large = jnp.minimum(large, half - 1)
    return ret + jnp.where(n < max_exact, n, large)


def windowed_attention(q, k, v, sink, rel_bias):
    B, S = q.shape[:2]
    nb = S // BLOCK
    qb = q.reshape(B, nb, BLOCK, N_KV_HEADS, GROUP, HEAD_DIM)

    def kv_windows(t):
        tp = jnp.pad(t, ((0, 0), (BLOCK, BLOCK), (0, 0), (0, 0))).reshape(B, nb + 2, BLOCK, N_KV_HEADS, HEAD_DIM)
        return jnp.concatenate([tp[:, :-2], tp[:, 1:-1], tp[:, 2:]], axis=2)

    kw = kv_windows(k)
    vw = kv_windows(v)
    s = jnp.einsum('bnqhgd,bnkhd->bnhgqk', qb, kw, preferred_element_type=jnp.float32) * (HEAD_DIM ** -0.5)
    qi = jnp.arange(BLOCK)[:, None]
    kj = jnp.arange(3 * BLOCK)[None, :]
    rel = kj - BLOCK - qi
    bias = rel_bias.astype(jnp.float32)[t5_relative_bucket(rel)]
    bias = bias.transpose(2, 0, 1).reshape(N_KV_HEADS, GROUP, BLOCK, 3 * BLOCK)
    kpos = (jnp.arange(nb)[:, None] - 1) * BLOCK + kj
    valid = (kpos >= 0) & (kpos < S)
    mask = (jnp.abs(rel) <= WINDOW)[None] & valid[:, None, :]
    s = jnp.where(mask[None, :, None, None], s + bias, -1e30)
    sk = sink.astype(jnp.float32).reshape(N_KV_HEADS, GROUP, 1, 1)
    m = jnp.maximum(jnp.max(s, axis=-1, keepdims=True), sk)
    p = jnp.exp(s - m)
    p = p / (jnp.sum(p, axis=-1, keepdims=True) + jnp.exp(sk - m))
    o = jnp.einsum('bnhgqk,bnkhd->bnqhgd', p.astype(v.dtype), vw)
    return o.reshape(B, S, ATTN_W)


def depthwise_conv(x, w, b):
    C = x.shape[-1]
    y = lax.conv_general_dilated(x, w.astype(x.dtype)[:, None, :], window_strides=(1,), padding=[(2, 1)],
                                 dimension_numbers=('NWC', 'WIO', 'NWC'), feature_group_count=C)
    return y + b.astype(x.dtype)


def _lin_combine(left, right):
    a1, b1 = left
    a2, b2 = right
    return a1 * a2, a2 * b1 + b2


def rg_lru_bidir(xc, w_r, b_r, w_i, b_i, lam):
    B, S, _ = xc.shape
    xf = xc.astype(jnp.float32)
    xh = xf.reshape(B, S, LRU_BLOCKS, LRU_BW)
    r = jax.nn.sigmoid(jnp.einsum('bshi,dhij->dbshj', xh, w_r.astype(jnp.float32)).reshape(2, B, S, LRU_W)
                       + b_r.astype(jnp.float32)[:, None, None])
    ig = jax.nn.sigmoid(jnp.einsum('bshi,dhij->dbshj', xh, w_i.astype(jnp.float32)).reshape(2, B, S, LRU_W)
                        + b_i.astype(jnp.float32)[:, None, None])
    log_a = -LRU_C * r * jax.nn.softplus(-lam.astype(jnp.float32))[:, None, None]
    a = jnp.exp(log_a)
    u = jnp.sqrt(-jnp.expm1(2.0 * log_a)) * (ig * xf[None])
    _, h_fwd = lax.associative_scan(_lin_combine, (a[0], u[0]), axis=1)
    _, h_bwd = lax.associative_scan(_lin_combine, (a[1], u[1]), axis=1, reverse=True)
    return (h_fwd + h_bwd).astype(xc.dtype)


def parallel_mixer(h, w_in, conv_w, conv_b, lru_w_r, lru_b_r, lru_w_i, lru_b_i, lru_lambda,
                   attn_sink, rel_bias, attn_norm_g, lru_norm_g, w_out):
    B, S, _ = h.shape
    proj = jnp.matmul(h, w_in)
    q, k, v, xr, xg = jnp.split(proj, [ATTN_W, ATTN_W + KV_W, ATTN_W + 2 * KV_W, ATTN_W + 2 * KV_W + LRU_W], axis=-1)
    attn = windowed_attention(q.reshape(B, S, N_Q_HEADS, HEAD_DIM), k.reshape(B, S, N_KV_HEADS, HEAD_DIM),
                              v.reshape(B, S, N_KV_HEADS, HEAD_DIM), attn_sink, rel_bias)
    xc = depthwise_conv(xr, conv_w, conv_b)
    lru = rg_lru_bidir(xc, lru_w_r, lru_b_r, lru_w_i, lru_b_i, lru_lambda) * jax.nn.gelu(xg)
    merged = jnp.concatenate([rms_norm(attn, attn_norm_g), rms_norm(lru, lru_norm_g)], axis=-1)
    return jnp.matmul(merged, w_out)


def moe_ffn(h, router_w, router_b, w_gate_up, b_gate_up, w_down, b_down):
    B, S, D = h.shape
    T = B * S
    xf = h.reshape(T, D)
    logits = jnp.matmul(xf, router_w, preferred_element_type=jnp.float32) + router_b.astype(jnp.float32)
    top_v, top_e = lax.top_k(logits, TOP_K)
    top_w = jax.nn.softmax(top_v, axis=-1).astype(h.dtype)
    flat_e = top_e.reshape(-1)
    order = jnp.argsort(flat_e)
    sorted_e = flat_e[order]
    counts = jnp.bincount(flat_e, length=N_EXPERTS)
    padded = (counts + MOE_BLOCK - 1) // MOE_BLOCK * MOE_BLOCK
    pad_end = jnp.cumsum(padded)
    pad_start = pad_end - padded
    start = jnp.cumsum(counts) - counts
    rank = jnp.arange(T * TOP_K, dtype=jnp.int32) - start[sorted_e]
    dest = pad_start[sorted_e] + rank
    n_rows = -(-(T * TOP_K + N_EXPERTS * (MOE_BLOCK - 1)) // MOE_BLOCK) * MOE_BLOCK
    n_blocks = n_rows // MOE_BLOCK
    row_token = jnp.full((n_rows,), T, jnp.int32).at[dest].set((order // TOP_K).astype(jnp.int32))
    row_weight = jnp.zeros((n_rows,), h.dtype).at[dest].set(top_w.reshape(-1)[order])
    block_e = jnp.minimum(jnp.searchsorted(pad_end, jnp.arange(n_blocks, dtype=jnp.int32) * MOE_BLOCK, side='right'),
                          N_EXPERTS - 1)
    x_pad = jnp.concatenate([xf, jnp.zeros((1, D), h.dtype)], axis=0)
    xb = x_pad[row_token].reshape(n_blocks, MOE_BLOCK, D)

    def expert_block(args):
        xblk, e = args
        hgu = jnp.matmul(xblk, w_gate_up[e]) + b_gate_up[e]
        gate = jnp.minimum(hgu[:, 0::2], SWIGLU_LIMIT)
        up = jnp.clip(hgu[:, 1::2], -SWIGLU_LIMIT, SWIGLU_LIMIT)
        act = gate * jax.nn.sigmoid(SWIGLU_ALPHA * gate) * (up + 1.0)
        return jnp.matmul(act, w_down[e]) + b_down[e]

    yb = lax.map(expert_block, (xb, block_e)).reshape(n_rows, D)
    out = jnp.zeros((T + 1, D), yb.dtype).at[row_token].add(yb * row_weight[:, None].astype(yb.dtype))
    return out[:T].reshape(B, S, D).astype(h.dtype)


def encoder_trunk(x, c, w_ada, b_ada, w_in, conv_w, conv_b, lru_w_r, lru_b_r, lru_w_i, lru_b_i, lru_lambda,
                  attn_sink, rel_bias, attn_norm_g, lru_norm_g, w_out, ln1_g, ln1_b,
                  router_w, router_b, w_gate_up, b_gate_up, w_down, b_down, ln2_g, ln2_b):
    for l in range(DEPTH):
        mod = jnp.matmul(jax.nn.silu(c), w_ada[l]) + b_ada[l]
        sh1, sc1, g1, sh2, sc2, g2 = jnp.split(mod[:, None, :], N_MOD, axis=-1)
        h = x * (1.0 + sc1) + sh1
        y = parallel_mixer(h, w_in[l], conv_w[l], conv_b[l], lru_w_r[l], lru_b_r[l], lru_w_i[l], lru_b_i[l],
                           lru_lambda[l], attn_sink[l], rel_bias, attn_norm_g[l], lru_norm_g[l], w_out[l])
        x = layer_norm(DEEPNORM_ALPHA * x + g1 * y, ln1_g[l], ln1_b[l])
        h = x * (1.0 + sc2) + sh2
        y = moe_ffn(h, router_w[l], router_b[l], w_gate_up[l], b_gate_up[l], w_down[l], b_down[l])
        x = layer_norm(DEEPNORM_ALPHA * x + g2 * y, ln2_g[l], ln2_b[l])
    return x


def setup_inputs(seed: int = 0) -> dict:
    key = jax.random.key(seed)
    ks = jax.random.split(key, 32)
    f32 = jnp.float32

    def nrm(k, shape, s):
        return jax.random.normal(k, shape, f32) * s

    u = jax.random.uniform(ks[10], (DEPTH, 2, LRU_W), f32, 0.9, 0.999)
    a = u ** (1.0 / LRU_C)
    lru_lambda = jnp.log(a) - jnp.log1p(-a)
    return {
        'x_prompt': nrm(ks[0], (BATCH, SEQ, D_MODEL), 1.0),
        'x_sample': nrm(ks[1], (DEC_BATCH, DEC_SEQ, D_MODEL), 1.0),
        'c_prompt': nrm(ks[2], (BATCH, D_MODEL), 1.0),
        'c_sample': nrm(ks[3], (DEC_BATCH, D_MODEL), 1.0),
        'w_ada': nrm(ks[4], (DEPTH, D_MODEL, N_MOD * D_MODEL), 0.5 * D_MODEL ** -0.5),
        'b_ada': nrm(ks[5], (DEPTH, N_MOD * D_MODEL), 0.01),
        'w_in': nrm(ks[6], (DEPTH, D_MODEL, PROJ_W), D_MODEL ** -0.5),
        'conv_w': nrm(ks[7], (DEPTH, CONV_W, LRU_W), CONV_W ** -0.5),
        'conv_b': nrm(ks[8], (DEPTH, LRU_W), 0.01),
        'lru_w_r': nrm(ks[9], (DEPTH, 2, LRU_BLOCKS, LRU_BW, LRU_BW), LRU_BW ** -0.5),
        'lru_b_r': nrm(ks[11], (DEPTH, 2, LRU_W), 0.01),
        'lru_w_i': nrm(ks[12], (DEPTH, 2, LRU_BLOCKS, LRU_BW, LRU_BW), LRU_BW ** -0.5),
        'lru_b_i': nrm(ks[13], (DEPTH, 2, LRU_W), 0.01),
        'lru_lambda': lru_lambda,
        'attn_sink': nrm(ks[14], (DEPTH, N_Q_HEADS), 0.5),
        'rel_bias': nrm(ks[15], (N_BUCKETS, N_Q_HEADS), 0.1),
        'attn_norm_g': 1.0 + nrm(ks[16], (DEPTH, ATTN_W), 0.01),
        'lru_norm_g': 1.0 + nrm(ks[17], (DEPTH, LRU_W), 0.01),
        'w_out': nrm(ks[18], (DEPTH, D_MIX, D_MODEL), DEEPNORM_BETA * D_MIX ** -0.5),
        'ln1_g': 1.0 + nrm(ks[19], (DEPTH, D_MODEL), 0.01),
        'ln1_b': nrm(ks[20], (DEPTH, D_MODEL), 0.01),
        'router_w': nrm(ks[21], (DEPTH, D_MODEL, N_EXPERTS), D_MODEL ** -0.5),
        'router_b': nrm(ks[22], (DEPTH, N_EXPERTS), 0.01),
        'w_gate_up': nrm(ks[23], (DEPTH, N_EXPERTS, D_MODEL, 2 * D_FF), D_MODEL ** -0.5),
        'b_gate_up': nrm(ks[24], (DEPTH, N_EXPERTS, 2 * D_FF), 0.01),
        'w_down': nrm(ks[25], (DEPTH, N_EXPERTS, D_FF, D_MODEL), DEEPNORM_BETA * D_FF ** -0.5),
        'b_down': nrm(ks[26], (DEPTH, N_EXPERTS, D_MODEL), 0.01),
        'ln2_g': 1.0 + nrm(ks[27], (DEPTH, D_MODEL), 0.01),
        'ln2_b': nrm(ks[28], (DEPTH, D_MODEL), 0.01),
    }


def reference(x_prompt, x_sample, c_prompt, c_sample, w_ada, b_ada, w_in, conv_w, conv_b,
              lru_w_r, lru_b_r, lru_w_i, lru_b_i, lru_lambda, attn_sink, rel_bias, attn_norm_g, lru_norm_g,
              w_out, ln1_g, ln1_b, router_w, router_b, w_gate_up, b_gate_up, w_down, b_down, ln2_g, ln2_b):
    y_prompt = encoder_trunk(x_prompt, c_prompt, w_ada, b_ada, w_in, conv_w, conv_b, lru_w_r, lru_b_r, lru_w_i,
                             lru_b_i, lru_lambda, attn_sink, rel_bias, attn_norm_g, lru_norm_g, w_out, ln1_g, ln1_b,
                             router_w, router_b, w_gate_up, b_gate_up, w_down, b_down, ln2_g, ln2_b)
    y_sample = encoder_trunk(x_sample, c_sample, w_ada, b_ada, w_in, conv_w, conv_b, lru_w_r, lru_b_r, lru_w_i,
                             lru_b_i, lru_lambda, attn_sink, rel_bias, attn_norm_g, lru_norm_g, w_out, ln1_g, ln1_b,
                             router_w, router_b, w_gate_up, b_gate_up, w_down, b_down, ln2_g, ln2_b)
    return (y_prompt, y_sample)
```

```python
import functools
import math

import jax
import jax.numpy as jnp
from jax import lax
from jax.experimental import pallas as pl
from jax.experimental.pallas import tpu as pltpu

F32 = jnp.float32
BF16 = jnp.bfloat16
I32 = jnp.int32
U32 = jnp.uint32

HEAD_DIM = 128
BLOCK = 128
WINDOW = 128
MAX_DISTANCE = 128
CONV_W = 4
LRU_C = 8.0
TOP_K = 4
SWIGLU_LIMIT = 7.0
SWIGLU_ALPHA = 1.702
EPS = 1e-5
NEG = -1e30
LANES = 128
SUBLANES = 8
LANE_BITS = 7
TOPK_BITS = 2
assert 1 << LANE_BITS == LANES and 1 << TOPK_BITS == TOP_K
VMEM_LIMIT = 56 * 1024 * 1024


def _cp(sem, vmem=VMEM_LIMIT):
    return pltpu.CompilerParams(dimension_semantics=sem, vmem_limit_bytes=vmem)


def _tile(n, want):
    t = min(n, want)
    while n % t:
        t -= 8
    return t


def _ada_kernel(c_ref, w_ref, b_ref, o_ref, acc_ref, *, tk, nk):
    k = pl.program_id(1)

    @pl.when(k == 0)
    def _():
        acc_ref[...] = jnp.zeros_like(acc_ref)

    c = c_ref[...]
    cs = c * jax.nn.sigmoid(c)
    tn = w_ref.shape[1]
    a0 = acc_ref[0]
    a1 = acc_ref[1]
    for r in range(tk // SUBLANES):
        w = w_ref[r * SUBLANES:(r + 1) * SUBLANES, :]
        c8 = cs[r * SUBLANES:(r + 1) * SUBLANES, :]
        a0 = a0 + w * jnp.broadcast_to(c8[:, 0:1], (SUBLANES, tn))
        a1 = a1 + w * jnp.broadcast_to(c8[:, 1:2], (SUBLANES, tn))
    acc_ref[0] = a0
    acc_ref[1] = a1

    @pl.when(k == nk - 1)
    def _():
        r0 = jnp.sum(acc_ref[0], axis=0, keepdims=True)
        r1 = jnp.sum(acc_ref[1], axis=0, keepdims=True)
        o_ref[...] = jnp.concatenate([r0, r1], axis=0) + b_ref[...]


def _ada(c2, w, b):
    d, n = w.shape
    tk = _tile(d, 512)
    tn = _tile(n, 1024)
    nk = d // tk
    return pl.pallas_call(
        functools.partial(_ada_kernel, tk=tk, nk=nk),
        grid=(n // tn, nk),
        in_specs=[pl.BlockSpec((tk, 2), lambda j, k: (k, 0)),
                  pl.BlockSpec((tk, tn), lambda j, k: (k, j)),
                  pl.BlockSpec((1, tn), lambda j, k: (0, j))],
        out_specs=pl.BlockSpec((2, tn), lambda j, k: (0, j)),
        out_shape=jax.ShapeDtypeStruct((2, n), F32),
        scratch_shapes=[pltpu.VMEM((2, SUBLANES, tn), F32)],
        compiler_params=_cp(("parallel", "arbitrary")),
        name="ada_mod",
    )(c2, w, b)


def _inproj_kernel(x_ref, sc_ref, sh_ref, w_ref, o_ref, hb_ref):
    @pl.when(pl.program_id(1) == 0)
    def _():
        hb_ref[...] = (x_ref[...] * (1.0 + sc_ref[...]) + sh_ref[...]).astype(BF16)

    o_ref[...] = jnp.dot(hb_ref[...], w_ref[...], preferred_element_type=F32).astype(o_ref.dtype)


def _inproj(x, sc, sh, w_bf, col0, ncols, out_dtype, tm, tn):
    s, d = x.shape
    off = col0 // tn
    return pl.pallas_call(
        _inproj_kernel,
        grid=(s // tm, ncols // tn),
        in_specs=[pl.BlockSpec((tm, d), lambda i, j: (i, 0)),
                  pl.BlockSpec((1, d), lambda i, j: (0, 0)),
                  pl.BlockSpec((1, d), lambda i, j: (0, 0)),
                  pl.BlockSpec((d, tn), lambda i, j: (0, j + off))],
        out_specs=pl.BlockSpec((tm, tn), lambda i, j: (i, j)),
        out_shape=jax.ShapeDtypeStruct((s, ncols), out_dtype),
        scratch_shapes=[pltpu.VMEM((tm, d), BF16)],
        compiler_params=_cp(("parallel", "arbitrary")),
        name="in_proj",
    )(x, sc, sh, w_bf)


def _attn_kernel(sink_ref, q_ref, kp_ref, kc_ref, kn_ref, vp_ref, vc_ref, vn_ref, bias_ref, g_ref, o_ref,
                 *, seq, n_kv, group, scale):
    i = pl.program_id(0)
    kw = jnp.concatenate([kp_ref[...], kc_ref[...], kn_ref[...]], axis=0)
    vw = jnp.concatenate([vp_ref[...], vc_ref[...], vn_ref[...]], axis=0)
    qi = lax.broadcasted_iota(I32, (BLOCK, 3 * BLOCK), 0)
    kj = lax.broadcasted_iota(I32, (BLOCK, 3 * BLOCK), 1)
    rel = kj - BLOCK - qi
    kpos = (i - 1) * BLOCK + kj
    mask = (jnp.abs(rel) <= WINDOW) & (kpos >= 0) & (kpos < seq)
    outs = []
    for h in range(n_kv):
        kh = kw[:, h * HEAD_DIM:(h + 1) * HEAD_DIM]
        vh = vw[:, h * HEAD_DIM:(h + 1) * HEAD_DIM]
        for g in range(group):
            hq = h * group + g
            qh = q_ref[:, hq * HEAD_DIM:(hq + 1) * HEAD_DIM]
            s = lax.dot_general(qh, kh, (((1,), (1,)), ((), ())), preferred_element_type=F32) * scale
            s = jnp.where(mask, s + bias_ref[hq], NEG)
            sk = sink_ref[hq]
            m = jnp.maximum(jnp.max(s, axis=-1, keepdims=True), sk)
            p = jnp.exp(s - m)
            p = p / (jnp.sum(p, axis=-1, keepdims=True) + jnp.exp(sk - m))
            outs.append(jnp.dot(p.astype(BF16), vh, preferred_element_type=F32))
    o = jnp.concatenate(outs, axis=-1)
    y = o * lax.rsqrt(jnp.mean(jnp.square(o), axis=-1, keepdims=True) + EPS)
    o_ref[...] = (y * g_ref[...]).astype(o_ref.dtype)


def _attention(qkv, sink, bias, g, attn_w, kv_w):
    s = qkv.shape[0]
    nb = s // BLOCK
    n_kv = kv_w // HEAD_DIM
    n_q = attn_w // HEAD_DIM
    kcol = attn_w // kv_w
    vcol = kcol + 1
    prev = lambda i: jnp.maximum(i - 1, 0)
    nxt = lambda i: jnp.minimum(i + 1, nb - 1)
    kern = functools.partial(_attn_kernel, seq=s, n_kv=n_kv, group=n_q // n_kv, scale=HEAD_DIM ** -0.5)
    return pl.pallas_call(
        kern,
        grid=(nb,),
        in_specs=[pl.BlockSpec(memory_space=pltpu.SMEM),
                  pl.BlockSpec((BLOCK, attn_w), lambda i: (i, 0)),
                  pl.BlockSpec((BLOCK, kv_w), lambda i: (prev(i), kcol)),
                  pl.BlockSpec((BLOCK, kv_w), lambda i: (i, kcol)),
                  pl.BlockSpec((BLOCK, kv_w), lambda i: (nxt(i), kcol)),
                  pl.BlockSpec((BLOCK, kv_w), lambda i: (prev(i), vcol)),
                  pl.BlockSpec((BLOCK, kv_w), lambda i: (i, vcol)),
                  pl.BlockSpec((BLOCK, kv_w), lambda i: (nxt(i), vcol)),
                  pl.BlockSpec((n_q, BLOCK, 3 * BLOCK), lambda i: (0, 0, 0)),
                  pl.BlockSpec((1, attn_w), lambda i: (0, 0))],
        out_specs=pl.BlockSpec((BLOCK, attn_w), lambda i: (i, 0)),
        out_shape=jax.ShapeDtypeStruct((s, attn_w), BF16),
        compiler_params=_cp(("parallel",)),
        name="swa_attn",
    )(sink, qkv, qkv, qkv, qkv, qkv, qkv, qkv, bias, g)


def _t5_bucket(rel, n_buckets):
    half = n_buckets // 2
    max_exact = half // 2
    ret = jnp.where(rel > 0, half, 0)
    n = jnp.abs(rel)
    nf = jnp.maximum(n, 1).astype(F32)
    large = max_exact + (jnp.log(nf / max_exact) / math.log(MAX_DISTANCE / max_exact)
                         * (half - max_exact)).astype(I32)
    large = jnp.minimum(large, half - 1)
    return ret + jnp.where(n < max_exact, n, large)


def _bias_table(rel_bias):
    qi = jnp.arange(BLOCK)[:, None]
    kj = jnp.arange(3 * BLOCK)[None, :]
    rel = kj - BLOCK - qi
    return rel_bias.astype(F32)[_t5_bucket(rel, rel_bias.shape[0])].transpose(2, 0, 1)


def _lru_gates(xp_ref, xc_ref, xn_ref, cw_ref, cb_ref, wg_ref, bg_ref, sp_ref, a_ref, u_ref, *, first, last,
               n_blk):
    tr, w = xc_ref.shape
    bw = w // n_blk
    xp = jnp.where(first, 0.0, xp_ref[...])
    xn = jnp.where(last, 0.0, xn_ref[...])
    ext = jnp.concatenate([xp, xc_ref[...], xn], axis=0)
    n_ext = tr + 2 * SUBLANES
    cw = cw_ref[...]
    xc = (cw[0:1] * pltpu.roll(ext, 2, axis=0)[SUBLANES:SUBLANES + tr]
          + cw[1:2] * pltpu.roll(ext, 1, axis=0)[SUBLANES:SUBLANES + tr]
          + cw[2:3] * ext[SUBLANES:SUBLANES + tr]
          + cw[3:4] * pltpu.roll(ext, n_ext - 1, axis=0)[SUBLANES:SUBLANES + tr]) + cb_ref[...]
    xcb = xc.astype(BF16)
    for h in range(n_blk):
        sl = slice(h * bw, (h + 1) * bw)
        gi = jnp.dot(xcb[:, sl], wg_ref[h], preferred_element_type=F32) + bg_ref[h]
        r = jax.nn.sigmoid(gi[:, :bw])
        ig = jax.nn.sigmoid(gi[:, bw:])
        log_a = (-LRU_C) * r * sp_ref[:, sl]
        a_ref[:, sl] = jnp.exp(log_a)
        u_ref[:, sl] = jnp.sqrt(1.0 - jnp.exp(2.0 * log_a)) * (ig * xc[:, sl])


def _lru_fwd_kernel(xp_ref, xc_ref, xn_ref, cw_ref, cb_ref, wg_ref, bg_ref, sp_ref, o_ref,
                    a_ref, u_ref, carry_ref, *, nt, n_blk):
    i = pl.program_id(0)

    @pl.when(i == 0)
    def _():
        carry_ref[...] = jnp.zeros_like(carry_ref)

    _lru_gates(xp_ref, xc_ref, xn_ref, cw_ref, cb_ref, wg_ref, bg_ref, sp_ref, a_ref, u_ref,
               first=i == 0, last=i == nt - 1, n_blk=n_blk)
    tr = xc_ref.shape[0]

    def step(t, h):
        h = a_ref[pl.ds(t, 1), :] * h + u_ref[pl.ds(t, 1), :]
        o_ref[pl.ds(t, 1), :] = h
        return h

    carry_ref[...] = lax.fori_loop(0, tr, step, carry_ref[...], unroll=8)


def _lru_bwd_kernel(xp_ref, xc_ref, xn_ref, cw_ref, cb_ref, wg_ref, bg_ref, sp_ref, hf_ref, xg_ref, g_ref,
                    o_ref, a_ref, u_ref, hb_ref, carry_ref, *, nt, n_blk):
    i = pl.program_id(0)

    @pl.when(i == 0)
    def _():
        carry_ref[...] = jnp.zeros_like(carry_ref)

    _lru_gates(xp_ref, xc_ref, xn_ref, cw_ref, cb_ref, wg_ref, bg_ref, sp_ref, a_ref, u_ref,
               first=i == nt - 1, last=i == 0, n_blk=n_blk)
    tr = xc_ref.shape[0]

    def step(t, h):
        row = tr - 1 - t
        h = a_ref[pl.ds(row, 1), :] * h + u_ref[pl.ds(row, 1), :]
        hb_ref[pl.ds(row, 1), :] = h
        return h

    carry_ref[...] = lax.fori_loop(0, tr, step, carry_ref[...], unroll=8)
    lru = (hf_ref[...] + hb_ref[...]) * jax.nn.gelu(xg_ref[...])
    y = lru * lax.rsqrt(jnp.mean(jnp.square(lru), axis=-1, keepdims=True) + EPS)
    o_ref[...] = (y * g_ref[...]).astype(o_ref.dtype)


def _lru(xrg, cw, cb, wg, bg, sp, g, lru_w, tr):
    s = xrg.shape[0]
    nt = s // tr
    n_blk = wg.shape[1]
    hb8 = tr // SUBLANES
    n8 = s // SUBLANES

    def halo_specs(tile):
        return [pl.BlockSpec((SUBLANES, lru_w), lambda i: (jnp.maximum(tile(i) * hb8 - 1, 0), 0)),
                pl.BlockSpec((tr, lru_w), lambda i: (tile(i), 0)),
                pl.BlockSpec((SUBLANES, lru_w), lambda i: (jnp.minimum((tile(i) + 1) * hb8, n8 - 1), 0))]

    def param_specs(d):
        return [pl.BlockSpec((CONV_W, lru_w), lambda i: (0, 0)),
                pl.BlockSpec((1, lru_w), lambda i: (0, 0)),
                pl.BlockSpec((None, n_blk, lru_w // n_blk, 2 * lru_w // n_blk), lambda i: (d, 0, 0, 0)),
                pl.BlockSpec((None, n_blk, 1, 2 * lru_w // n_blk), lambda i: (d, 0, 0, 0)),
                pl.BlockSpec((None, 1, lru_w), lambda i: (d, 0, 0))]

    fwd_tile = lambda i: i
    hf = pl.pallas_call(
        functools.partial(_lru_fwd_kernel, nt=nt, n_blk=n_blk),
        grid=(nt,),
        in_specs=halo_specs(fwd_tile) + param_specs(0),
        out_specs=pl.BlockSpec((tr, lru_w), lambda i: (i, 0)),
        out_shape=jax.ShapeDtypeStruct((s, lru_w), F32),
        scratch_shapes=[pltpu.VMEM((tr, lru_w), F32), pltpu.VMEM((tr, lru_w), F32), pltpu.VMEM((1, lru_w), F32)],
        compiler_params=_cp(("arbitrary",)),
        name="lru_fwd",
    )(xrg, xrg, xrg, cw, cb, wg, bg, sp)

    bwd_tile = lambda i: nt - 1 - i
    return pl.pallas_call(
        functools.partial(_lru_bwd_kernel, nt=nt, n_blk=n_blk),
        grid=(nt,),
        in_specs=halo_specs(bwd_tile) + param_specs(1) + [
            pl.BlockSpec((tr, lru_w), lambda i: (nt - 1 - i, 0)),
            pl.BlockSpec((tr, lru_w), lambda i: (nt - 1 - i, 1)),
            pl.BlockSpec((1, lru_w), lambda i: (0, 0))],
        out_specs=pl.BlockSpec((tr, lru_w), lambda i: (nt - 1 - i, 0)),
        out_shape=jax.ShapeDtypeStruct((s, lru_w), BF16),
        scratch_shapes=[pltpu.VMEM((tr, lru_w), F32), pltpu.VMEM((tr, lru_w), F32), pltpu.VMEM((tr, lru_w), F32),
                        pltpu.VMEM((1, lru_w), F32)],
        compiler_params=_cp(("arbitrary",)),
        name="lru_bwd",
    )(xrg, xrg, xrg, cw, cb, wg, bg, sp, hf, xrg, g)


def _outproj_kernel(a1_ref, a2_ref, w1_ref, w2_ref, o_ref):
    o_ref[...] = (jnp.dot(a1_ref[...], w1_ref[...], preferred_element_type=F32)
                  + jnp.dot(a2_ref[...], w2_ref[...], preferred_element_type=F32))


def _outproj(a1, a2, w_bf, tm, tn):
    s, k1 = a1.shape
    k2 = a2.shape[1]
    assert k1 == k2
    n = w_bf.shape[1]
    return pl.pallas_call(
        _outproj_kernel,
        grid=(s // tm, n // tn),
        in_specs=[pl.BlockSpec((tm, k1), lambda i, j: (i, 0)),
                  pl.BlockSpec((tm, k2), lambda i, j: (i, 0)),
                  pl.BlockSpec((k1, tn), lambda i, j: (0, j)),
                  pl.BlockSpec((k2, tn), lambda i, j: (1, j))],
        out_specs=pl.BlockSpec((tm, tn), lambda i, j: (i, j)),
        out_shape=jax.ShapeDtypeStruct((s, n), F32),
        compiler_params=_cp(("parallel", "arbitrary")),
        name="out_proj",
    )(a1, a2, w_bf, w_bf)


def _pack_bf16_pair(hi, lo):
    hb = pltpu.bitcast(hi.astype(BF16).astype(F32), U32)
    lb = pltpu.bitcast(lo.astype(BF16).astype(F32), U32)
    return (hb & jnp.uint32(0xFFFF0000)) | (lb >> 16)


def _unpack_bf16_pair(p):
    hi = pltpu.bitcast(p & jnp.uint32(0xFFFF0000), F32).astype(BF16)
    lo = pltpu.bitcast(p << 16, F32).astype(BF16)
    return hi, lo


def _ln_router_kernel(x_ref, y_ref, g1_ref, lg_ref, lb_ref, sc_ref, sh_ref, rw_ref, rb_ref, cin_ref,
                      x1_ref, hp_ref, route_ref, cout_ref, cnt_ref, *, alpha):
    i = pl.program_id(0)

    @pl.when(i == 0)
    def _():
        cnt_ref[...] = cin_ref[...]

    z = alpha * x_ref[...] + g1_ref[...] * y_ref[...]
    mu = jnp.mean(z, axis=-1, keepdims=True)
    var = jnp.mean(jnp.square(z - mu), axis=-1, keepdims=True)
    x1 = (z - mu) * lax.rsqrt(var + EPS) * lg_ref[...] + lb_ref[...]
    x1_ref[...] = x1
    h = x1 * (1.0 + sc_ref[...]) + sh_ref[...]
    half = h.shape[1] // 2
    hp_ref[...] = _pack_bf16_pair(h[:, :half], h[:, half:])

    logits = jnp.dot(h, rw_ref[...], preferred_element_type=F32, precision=lax.Precision.HIGHEST) + rb_ref[...]
    tm = logits.shape[0]
    lane = lax.broadcasted_iota(I32, (tm, LANES), 1).astype(F32)
    ri = lax.broadcasted_iota(I32, (tm, tm), 0)
    ci = lax.broadcasted_iota(I32, (tm, tm), 1)
    ltri = jnp.where(ci < ri, 1.0, 0.0).astype(BF16)
    counts = cnt_ref[...]
    route = jnp.zeros((tm, LANES), F32)
    vals = []
    l = logits
    for k in range(TOP_K):
        m = jnp.max(l, axis=-1, keepdims=True)
        idx = jnp.min(jnp.where(l == m, lane, float(LANES)), axis=-1, keepdims=True)
        oh = lane == idx
        l = jnp.where(oh, NEG * 2.0, l)
        ohf = jnp.where(oh, 1.0, 0.0)
        before = jnp.dot(ltri, ohf.astype(BF16), preferred_element_type=F32) + counts
        rank = jnp.sum(ohf * before, axis=-1, keepdims=True)
        counts = counts + jnp.sum(ohf, axis=0, keepdims=True)
        vals.append(m)
        route = route + jnp.where(lane == float(k), idx, 0.0) + jnp.where(lane == float(2 * TOP_K + k), rank, 0.0)
    es = [jnp.exp(v - vals[0]) for v in vals]
    den = es[0] + es[1] + es[2] + es[3]
    for k in range(TOP_K):
        route = route + jnp.where(lane == float(TOP_K + k), es[k] / den, 0.0)
    route_ref[...] = route
    cnt_ref[...] = counts
    cout_ref[...] = counts


def _ln_router(x, y, g1, lg, lb, sc, sh, rw, rb, cin, tm):
    s, d = x.shape
    row = pl.BlockSpec((tm, d), lambda i: (i, 0))
    vec = pl.BlockSpec((1, d), lambda i: (0, 0))
    return pl.pallas_call(
        functools.partial(_ln_router_kernel, alpha=2.0 ** 0.25),
        grid=(s // tm,),
        in_specs=[row, row, vec, vec, vec, vec, vec,
                  pl.BlockSpec((d, LANES), lambda i: (0, 0)),
                  pl.BlockSpec((1, LANES), lambda i: (0, 0)),
                  pl.BlockSpec((1, LANES), lambda i: (0, 0))],
        out_specs=[row,
                   pl.BlockSpec((tm, d // 2), lambda i: (i, 0)),
                   pl.BlockSpec((tm, LANES), lambda i: (i, 0)),
                   pl.BlockSpec((1, LANES), lambda i: (0, 0))],
        out_shape=[jax.ShapeDtypeStruct((s, d), F32),
                   jax.ShapeDtypeStruct((s, d // 2), U32),
                   jax.ShapeDtypeStruct((s, LANES), F32),
                   jax.ShapeDtypeStruct((1, LANES), F32)],
        scratch_shapes=[pltpu.VMEM((1, LANES), F32)],
        compiler_params=_cp(("arbitrary",)),
        name="ln_router",
    )(x, y, g1, lg, lb, sc, sh, rw, rb, cin)


def _dispatch_kernel(pos_hbm, h_ref, xs_hbm, pos_smem, sem, psem, *, td):
    i = pl.program_id(0)
    nrow = td * TOP_K // LANES
    cp = pltpu.make_async_copy(pos_hbm.at[pl.ds(i * nrow, nrow)], pos_smem, psem)
    cp.start()
    cp.wait()

    def row_copy(r):
        p = pos_smem[r >> LANE_BITS, r & (LANES - 1)]
        return pltpu.make_async_copy(h_ref.at[pl.ds(r >> TOPK_BITS, 1)], xs_hbm.at[pl.ds(p, 1)], sem)

    def issue(r, c):
        row_copy(r).start()
        return c

    lax.fori_loop(0, td * TOP_K, issue, 0)

    def drain(r, c):
        row_copy(r).wait()
        return c

    lax.fori_loop(0, td * TOP_K, drain, 0)


def _dispatch(pos2d, hp, n_rows, td):
    t, dh = hp.shape
    return pl.pallas_call(
        functools.partial(_dispatch_kernel, td=td),
        grid=(t // td,),
        in_specs=[pl.BlockSpec(memory_space=pl.ANY),
                  pl.BlockSpec((td, dh), lambda i: (i, 0))],
        out_specs=pl.BlockSpec(memory_space=pl.ANY),
        out_shape=jax.ShapeDtypeStruct((n_rows, dh), U32),
        scratch_shapes=[pltpu.SMEM((td * TOP_K // LANES, LANES), I32),
                        pltpu.SemaphoreType.DMA, pltpu.SemaphoreType.DMA],
        compiler_params=_cp(("arbitrary",)),
        name="moe_dispatch",
    )(pos2d, hp)


def _moe_gu_kernel(ce_ref, cr_ref, nc_ref, x_ref, w_ref, b_ref, psel_ref, o_ref, xb_ref, wb_ref, *, sb, nsb):
    c = pl.program_id(0)
    j = pl.program_id(1)
    rows = cr_ref[c]
    half = x_ref.shape[1]
    tn = w_ref.shape[2]

    @pl.when(c < nc_ref[0])
    def _():
        @pl.when(j == 0)
        def _():
            hi, lo = _unpack_bf16_pair(x_ref[...])
            xb_ref[:, :half] = hi
            xb_ref[:, half:] = lo

        wb_ref[...] = w_ref[0].astype(BF16)
        even = (lax.broadcasted_iota(I32, (sb, tn), 1) & 1) == 0
        for s in range(nsb):
            @pl.when(s * sb < rows)
            def _():
                h = jnp.dot(xb_ref[s * sb:(s + 1) * sb, :], wb_ref[...], preferred_element_type=F32) + b_ref[0]
                up = pltpu.roll(h, tn - 1, axis=1)
                gate = jnp.minimum(h, SWIGLU_LIMIT)
                up = jnp.clip(up, -SWIGLU_LIMIT, SWIGLU_LIMIT)
                act = gate * jax.nn.sigmoid(SWIGLU_ALPHA * gate) * (up + 1.0)
                act = jnp.where(even, act, 0.0).astype(BF16)
                o_ref[s * sb:(s + 1) * sb, :] = jnp.dot(act, psel_ref[...],
                                                        preferred_element_type=F32).astype(o_ref.dtype)


def _moe_gate_up(chunk_e, chunk_rows, n_chunks, xs, w_gu, b_gu, rc, sb, tn):
    n_rows, half = xs.shape
    n_exp, d, n2 = w_gu.shape
    nc_max = n_rows // rc
    nj = n2 // tn
    ceff = lambda c, nc: jnp.minimum(c, nc[0] - 1)
    jeff = lambda c, j, nc: jnp.where(c < nc[0], j, nj - 1)
    psel = (jnp.arange(tn)[:, None] == 2 * jnp.arange(tn // 2)[None, :]).astype(BF16)
    grid_spec = pltpu.PrefetchScalarGridSpec(
        num_scalar_prefetch=3,
        grid=(nc_max, nj),
        in_specs=[pl.BlockSpec((rc, half), lambda c, j, ce, cr, nc: (ceff(c, nc), 0)),
                  pl.BlockSpec((1, d, tn), lambda c, j, ce, cr, nc: (ce[ceff(c, nc)], 0, jeff(c, j, nc))),
                  pl.BlockSpec((1, 1, tn), lambda c, j, ce, cr, nc: (ce[ceff(c, nc)], 0, jeff(c, j, nc))),
                  pl.BlockSpec((tn, tn // 2), lambda c, j, ce, cr, nc: (0, 0))],
        out_specs=pl.BlockSpec((rc, tn // 2), lambda c, j, ce, cr, nc: (ceff(c, nc), jeff(c, j, nc))),
        scratch_shapes=[pltpu.VMEM((rc, d), BF16), pltpu.VMEM((d, tn), BF16)],
    )
    return pl.pallas_call(
        functools.partial(_moe_gu_kernel, sb=sb, nsb=rc // sb),
        grid_spec=grid_spec,
        out_shape=jax.ShapeDtypeStruct((n_rows, n2 // 2), BF16),
        compiler_params=_cp(("arbitrary", "arbitrary")),
        name="moe_gate_up",
    )(chunk_e, chunk_rows, n_chunks, xs, w_gu, b_gu.reshape(n_exp, 1, n2), psel)


def _moe_down_kernel(ce_ref, cr_ref, nc_ref, a_ref, w_ref, b_ref, o_ref, wb_ref, *, sb, nsb):
    c = pl.program_id(0)
    rows = cr_ref[c]

    @pl.when(c < nc_ref[0])
    def _():
        wb_ref[...] = w_ref[0].astype(BF16)
        for s in range(nsb):
            @pl.when(s * sb < rows)
            def _():
                o_ref[s * sb:(s + 1) * sb, :] = (
                    jnp.dot(a_ref[s * sb:(s + 1) * sb, :], wb_ref[...], preferred_element_type=F32) + b_ref[0])


def _moe_down(chunk_e, chunk_rows, n_chunks, act, w_d, b_d, rc, sb, tn):
    n_rows, f = act.shape
    n_exp, _, d = w_d.shape
    nc_max = n_rows // rc
    nj = d // tn
    ceff = lambda c, nc: jnp.minimum(c, nc[0] - 1)
    jeff = lambda c, j, nc: jnp.where(c < nc[0], j, nj - 1)
    grid_spec = pltpu.PrefetchScalarGridSpec(
        num_scalar_prefetch=3,
        grid=(nc_max, nj),
        in_specs=[pl.BlockSpec((rc, f), lambda c, j, ce, cr, nc: (ceff(c, nc), 0)),
                  pl.BlockSpec((1, f, tn), lambda c, j, ce, cr, nc: (ce[ceff(c, nc)], 0, jeff(c, j, nc))),
                  pl.BlockSpec((1, 1, tn), lambda c, j, ce, cr, nc: (ce[ceff(c, nc)], 0, jeff(c, j, nc)))],
        out_specs=pl.BlockSpec((rc, tn), lambda c, j, ce, cr, nc: (ceff(c, nc), jeff(c, j, nc))),
        scratch_shapes=[pltpu.VMEM((f, tn), BF16)],
    )
    return pl.pallas_call(
        functools.partial(_moe_down_kernel, sb=sb, nsb=rc // sb),
        grid_spec=grid_spec,
        out_shape=jax.ShapeDtypeStruct((n_rows, d), F32),
        compiler_params=_cp(("arbitrary", "arbitrary")),
        name="moe_down",
    )(chunk_e, chunk_rows, n_chunks, act, w_d, b_d.reshape(n_exp, 1, d))


def _combine_kernel(pos_hbm, ys_hbm, x_ref, rt_ref, g2_ref, lg_ref, lb_ref, o_ref, pos_smem, ybuf, sem, psem,
                    *, tc, blk0, alpha):
    i = pl.program_id(0)
    nrow = tc * TOP_K // LANES
    cp = pltpu.make_async_copy(pos_hbm.at[pl.ds((blk0 + i) * nrow, nrow)], pos_smem, psem)
    cp.start()
    cp.wait()

    def row_copy(r):
        p = pos_smem[r >> LANE_BITS, r & (LANES - 1)]
        return pltpu.make_async_copy(ys_hbm.at[pl.ds(p, 1)],
                                     ybuf.at[r & (TOP_K - 1), pl.ds(r >> TOPK_BITS, 1)], sem)

    def issue(r, c):
        row_copy(r).start()
        return c

    lax.fori_loop(0, tc * TOP_K, issue, 0)

    def drain(r, c):
        row_copy(r).wait()
        return c

    lax.fori_loop(0, tc * TOP_K, drain, 0)

    rt = rt_ref[...]
    y = rt[:, TOP_K:TOP_K + 1] * ybuf[0]
    for k in range(1, TOP_K):
        y = y + rt[:, TOP_K + k:TOP_K + k + 1] * ybuf[k]
    z = alpha * x_ref[...] + g2_ref[...] * y
    mu = jnp.mean(z, axis=-1, keepdims=True)
    var = jnp.mean(jnp.square(z - mu), axis=-1, keepdims=True)
    o_ref[...] = (z - mu) * lax.rsqrt(var + EPS) * lg_ref[...] + lb_ref[...]


def _combine(pos2d, ys, x1, route, g2, lg, lb, tok0, tc):
    s, d = x1.shape
    row = pl.BlockSpec((tc, d), lambda i: (i, 0))
    vec = pl.BlockSpec((1, d), lambda i: (0, 0))
    return pl.pallas_call(
        functools.partial(_combine_kernel, tc=tc, blk0=tok0 // tc, alpha=2.0 ** 0.25),
        grid=(s // tc,),
        in_specs=[pl.BlockSpec(memory_space=pl.ANY), pl.BlockSpec(memory_space=pl.ANY),
                  row, pl.BlockSpec((tc, LANES), lambda i: (i, 0)), vec, vec, vec],
        out_specs=row,
        out_shape=jax.ShapeDtypeStruct((s, d), F32),
        scratch_shapes=[pltpu.SMEM((tc * TOP_K // LANES, LANES), I32),
                        pltpu.VMEM((TOP_K, tc, d), F32),
                        pltpu.SemaphoreType.DMA, pltpu.SemaphoreType.DMA],
        compiler_params=_cp(("arbitrary",)),
        name="moe_combine",
    )(pos2d, ys, x1, route, g2, lg, lb)


def _tiles(d, seqs):
    smin = min(seqs)
    t_all = sum(seqs)
    big = d >= 2048
    return dict(
        tm_in=_tile(smin, 512), tn_in=512 if big else 256,
        tr_lru=_tile(smin, 256),
        tm_out=_tile(smin, 512), tn_out=1024 if big else 256,
        tm_ln=_tile(smin, 256),
        td=_tile(smin, 256), tc=_tile(smin, 256),
        rc=1024 if t_all * TOP_K >= 16384 else 128,
        sb=256 if t_all * TOP_K >= 16384 else 64,
        tn_gu=512 if big else 256, tn_dn=512 if big else 256,
    )


def kernel(x_prompt, x_sample, c_prompt, c_sample, w_ada, b_ada, w_in, conv_w, conv_b, lru_w_r, lru_b_r, lru_w_i,
           lru_b_i, lru_lambda, attn_sink, rel_bias, attn_norm_g, lru_norm_g, w_out, ln1_g, ln1_b, router_w,
           router_b, w_gate_up, b_gate_up, w_down, b_down, ln2_g, ln2_b):
    assert w_ada.shape[0] == 1, "single-layer trunk"
    d = x_prompt.shape[-1]
    xs_in = [x_prompt.reshape(-1, d), x_sample.reshape(-1, d)]
    assert x_prompt.shape[0] == 1 and x_sample.shape[0] == 1
    seqs = [x.shape[0] for x in xs_in]
    attn_w = d // 2
    lru_w = d - attn_w
    proj_w = w_in.shape[-1]
    kv_w = (proj_w - attn_w - 2 * lru_w) // 2
    n_exp = router_w.shape[-1]
    n_blk = lru_w_r.shape[2]
    tl = _tiles(d, seqs)
    row = lambda v: v.reshape(1, -1).astype(F32)

    c2 = jnp.concatenate([c_prompt, c_sample], axis=0).T
    mod = _ada(c2, w_ada[0], row(b_ada[0]))

    w_in_bf = w_in[0].astype(BF16)
    w_out_bf = w_out[0].astype(BF16)
    wg = jnp.concatenate([lru_w_r[0], lru_w_i[0]], axis=-1).astype(BF16)
    bw = lru_w // n_blk
    bg = jnp.concatenate([lru_b_r[0].reshape(2, n_blk, 1, bw), lru_b_i[0].reshape(2, n_blk, 1, bw)], axis=-1)
    sp = jax.nn.softplus(-lru_lambda[0].astype(F32)).reshape(2, 1, lru_w)
    bias = _bias_table(rel_bias)
    rw = jnp.zeros((d, LANES), F32).at[:, :n_exp].set(router_w[0])
    rb = jnp.full((1, LANES), NEG, F32).at[0, :n_exp].set(router_b[0])

    counts = jnp.zeros((1, LANES), F32)
    x1s, hps, routes = [], [], []
    for gi, x in enumerate(xs_in):
        m = [mod[gi:gi + 1, k * d:(k + 1) * d] for k in range(6)]
        sh1, sc1, g1, sh2, sc2, _ = m
        qkv = _inproj(x, sc1, sh1, w_in_bf, 0, attn_w + 2 * kv_w, BF16, tl["tm_in"], tl["tn_in"])
        xrg = _inproj(x, sc1, sh1, w_in_bf, attn_w + 2 * kv_w, 2 * lru_w, F32, tl["tm_in"], tl["tn_in"])
        attn = _attention(qkv, attn_sink[0].astype(F32), bias, row(attn_norm_g[0]), attn_w, kv_w)
        lru = _lru(xrg, conv_w[0].astype(F32), row(conv_b[0]), wg, bg, sp, row(lru_norm_g[0]), lru_w, tl["tr_lru"])
        y = _outproj(attn, lru, w_out_bf, tl["tm_out"], tl["tn_out"])
        x1, hp, route, counts = _ln_router(x, y, g1, row(ln1_g[0]), row(ln1_b[0]), sc2, sh2, rw, rb, counts,
                                           tl["tm_ln"])
        x1s.append(x1)
        hps.append(hp)
        routes.append(route)

    rc, sb = tl["rc"], tl["sb"]
    t_all = sum(seqs)
    route_all = jnp.concatenate(routes, axis=0)
    top_e = route_all[:, :TOP_K].astype(I32)
    rank = route_all[:, 2 * TOP_K:3 * TOP_K].astype(I32)
    cnt = counts[0, :n_exp].astype(I32)
    nchunk_e = (cnt + rc - 1) // rc
    chunk_end = jnp.cumsum(nchunk_e)
    chunk_start = chunk_end - nchunk_e
    pos = (chunk_start * rc)[top_e] + rank
    nc_max = t_all * TOP_K // rc + n_exp
    cidx = jnp.arange(nc_max, dtype=I32)
    chunk_e = jnp.minimum(jnp.searchsorted(chunk_end, cidx, side="right"), n_exp - 1).astype(I32)
    chunk_rows = jnp.clip(cnt[chunk_e] - (cidx - chunk_start[chunk_e]) * rc, 0, rc).astype(I32)
    n_chunks = chunk_end[-1:].astype(I32)
    pos2d = pos.reshape(-1, LANES).astype(I32)

    hp_all = jnp.concatenate(hps, axis=0)
    xs = _dispatch(pos2d, hp_all, nc_max * rc, tl["td"])
    act = _moe_gate_up(chunk_e, chunk_rows, n_chunks, xs, w_gate_up[0], b_gate_up[0], rc, sb, tl["tn_gu"])
    ys = _moe_down(chunk_e, chunk_rows, n_chunks, act, w_down[0], b_down[0], rc, sb, tl["tn_dn"])

    outs = []
    tok0 = 0
    for gi, x1 in enumerate(x1s):
        g2 = mod[gi:gi + 1, 5 * d:6 * d]
        out = _combine(pos2d, ys, x1, routes[gi], g2, row(ln2_g[0]), row(ln2_b[0]), tok0, tl["tc"])
        outs.append(out.reshape(1, seqs[gi], d))
        tok0 += seqs[gi]
    return tuple(outs)
```

```python
import functools
import math

import jax
import jax.numpy as jnp
from jax import lax
from jax.experimental import pallas as pl
from jax.experimental.pallas import tpu as pltpu

F32 = jnp.float32
BF16 = jnp.bfloat16
I32 = jnp.int32
U32 = jnp.uint32

HEAD_DIM = 128
BLOCK = 128
WINDOW = 128
MAX_DISTANCE = 128
CONV_W = 4
LRU_C = 8.0
TOP_K = 4
SWIGLU_LIMIT = 7.0
SWIGLU_ALPHA = 1.702
EPS = 1e-5
NEG = -1e30
LANES = 128
SUBLANES = 8
LANE_BITS = 7
TOPK_BITS = 2
assert 1 << LANE_BITS == LANES and 1 << TOPK_BITS == TOP_K
DMA_UNROLL = 8
VMEM_LIMIT = 56 * 1024 * 1024


def _cp(sem, vmem=VMEM_LIMIT):
    return pltpu.CompilerParams(dimension_semantics=sem, vmem_limit_bytes=vmem)


def _tile(n, want):
    t = min(n, want)
    while n % t:
        t -= 8
    return t


def _ada_kernel(c_ref, w_ref, b_ref, o_ref, acc_ref, *, tk, nk):
    k = pl.program_id(1)

    @pl.when(k == 0)
    def _():
        acc_ref[...] = jnp.zeros_like(acc_ref)

    c = c_ref[...]
    cs = c * jax.nn.sigmoid(c)
    tn = w_ref.shape[1]
    a0 = acc_ref[0]
    a1 = acc_ref[1]
    for r in range(tk // SUBLANES):
        w = w_ref[r * SUBLANES:(r + 1) * SUBLANES, :]
        c8 = cs[r * SUBLANES:(r + 1) * SUBLANES, :]
        a0 = a0 + w * jnp.broadcast_to(c8[:, 0:1], (SUBLANES, tn))
        a1 = a1 + w * jnp.broadcast_to(c8[:, 1:2], (SUBLANES, tn))
    acc_ref[0] = a0
    acc_ref[1] = a1

    @pl.when(k == nk - 1)
    def _():
        r0 = jnp.sum(acc_ref[0], axis=0, keepdims=True)
        r1 = jnp.sum(acc_ref[1], axis=0, keepdims=True)
        o_ref[...] = jnp.concatenate([r0, r1], axis=0) + b_ref[...]


def _ada(c2, w, b):
    d, n = w.shape
    tk = _tile(d, 512)
    tn = _tile(n, 1024)
    nk = d // tk
    return pl.pallas_call(
        functools.partial(_ada_kernel, tk=tk, nk=nk),
        grid=(n // tn, nk),
        in_specs=[pl.BlockSpec((tk, 2), lambda j, k: (k, 0)),
                  pl.BlockSpec((tk, tn), lambda j, k: (k, j)),
                  pl.BlockSpec((1, tn), lambda j, k: (0, j))],
        out_specs=pl.BlockSpec((2, tn), lambda j, k: (0, j)),
        out_shape=jax.ShapeDtypeStruct((2, n), F32),
        scratch_shapes=[pltpu.VMEM((2, SUBLANES, tn), F32)],
        compiler_params=_cp(("parallel", "arbitrary")),
        name="ada_mod",
    )(c2, w, b)


def _inproj_kernel(x_ref, sc_ref, sh_ref, w_ref, o_ref, hb_ref):
    @pl.when(pl.program_id(1) == 0)
    def _():
        hb_ref[...] = (x_ref[...] * (1.0 + sc_ref[...]) + sh_ref[...]).astype(BF16)

    o_ref[...] = jnp.dot(hb_ref[...], w_ref[...], preferred_element_type=F32).astype(o_ref.dtype)


def _inproj(x, sc, sh, w_bf, col0, ncols, out_dtype, tm, tn):
    s, d = x.shape
    off = col0 // tn
    return pl.pallas_call(
        _inproj_kernel,
        grid=(s // tm, ncols // tn),
        in_specs=[pl.BlockSpec((tm, d), lambda i, j: (i, 0)),
                  pl.BlockSpec((1, d), lambda i, j: (0, 0)),
                  pl.BlockSpec((1, d), lambda i, j: (0, 0)),
                  pl.BlockSpec((d, tn), lambda i, j: (0, j + off))],
        out_specs=pl.BlockSpec((tm, tn), lambda i, j: (i, j)),
        out_shape=jax.ShapeDtypeStruct((s, ncols), out_dtype),
        scratch_shapes=[pltpu.VMEM((tm, d), BF16)],
        compiler_params=_cp(("parallel", "arbitrary")),
        name="in_proj",
    )(x, sc, sh, w_bf)


def _attn_kernel(sink_ref, q_ref, kp_ref, kc_ref, kn_ref, vp_ref, vc_ref, vn_ref, bias_ref, g_ref, o_ref,
                 *, seq, n_kv, group, scale):
    i = pl.program_id(0)
    kw = jnp.concatenate([kp_ref[...], kc_ref[...], kn_ref[...]], axis=0)
    vw = jnp.concatenate([vp_ref[...], vc_ref[...], vn_ref[...]], axis=0)
    qi = lax.broadcasted_iota(I32, (BLOCK, 3 * BLOCK), 0)
    kj = lax.broadcasted_iota(I32, (BLOCK, 3 * BLOCK), 1)
    rel = kj - BLOCK - qi
    kpos = (i - 1) * BLOCK + kj
    mask = (jnp.abs(rel) <= WINDOW) & (kpos >= 0) & (kpos < seq)
    mask = jnp.concatenate([mask] * group, axis=0)
    outs = []
    for h in range(n_kv):
        kh = kw[:, h * HEAD_DIM:(h + 1) * HEAD_DIM]
        vh = vw[:, h * HEAD_DIM:(h + 1) * HEAD_DIM]
        heads = range(h * group, (h + 1) * group)
        qh = jnp.concatenate([q_ref[:, hq * HEAD_DIM:(hq + 1) * HEAD_DIM] for hq in heads], axis=0)
        sk = jnp.concatenate([jnp.full((BLOCK, 1), sink_ref[hq], F32) for hq in heads], axis=0)
        s = lax.dot_general(qh, kh, (((1,), (1,)), ((), ())), preferred_element_type=F32) * scale
        s = jnp.where(mask, s + bias_ref[h], NEG)
        m = jnp.maximum(jnp.max(s, axis=-1, keepdims=True), sk)
        p = jnp.exp(s - m)
        p = p / (jnp.sum(p, axis=-1, keepdims=True) + jnp.exp(sk - m))
        oh = jnp.dot(p.astype(BF16), vh, preferred_element_type=F32)
        outs += [oh[g * BLOCK:(g + 1) * BLOCK] for g in range(group)]
    o = jnp.concatenate(outs, axis=-1)
    y = o * lax.rsqrt(jnp.mean(jnp.square(o), axis=-1, keepdims=True) + EPS)
    o_ref[...] = (y * g_ref[...]).astype(o_ref.dtype)


def _attention(qkv, sink, bias, g, attn_w, kv_w):
    s = qkv.shape[0]
    nb = s // BLOCK
    n_kv = kv_w // HEAD_DIM
    n_q = attn_w // HEAD_DIM
    kcol = attn_w // kv_w
    vcol = kcol + 1
    prev = lambda i: jnp.maximum(i - 1, 0)
    nxt = lambda i: jnp.minimum(i + 1, nb - 1)
    kern = functools.partial(_attn_kernel, seq=s, n_kv=n_kv, group=n_q // n_kv, scale=HEAD_DIM ** -0.5)
    return pl.pallas_call(
        kern,
        grid=(nb,),
        in_specs=[pl.BlockSpec(memory_space=pltpu.SMEM),
                  pl.BlockSpec((BLOCK, attn_w), lambda i: (i, 0)),
                  pl.BlockSpec((BLOCK, kv_w), lambda i: (prev(i), kcol)),
                  pl.BlockSpec((BLOCK, kv_w), lambda i: (i, kcol)),
                  pl.BlockSpec((BLOCK, kv_w), lambda i: (nxt(i), kcol)),
                  pl.BlockSpec((BLOCK, kv_w), lambda i: (prev(i), vcol)),
                  pl.BlockSpec((BLOCK, kv_w), lambda i: (i, vcol)),
                  pl.BlockSpec((BLOCK, kv_w), lambda i: (nxt(i), vcol)),
                  pl.BlockSpec((n_kv, (n_q // n_kv) * BLOCK, 3 * BLOCK), lambda i: (0, 0, 0)),
                  pl.BlockSpec((1, attn_w), lambda i: (0, 0))],
        out_specs=pl.BlockSpec((BLOCK, attn_w), lambda i: (i, 0)),
        out_shape=jax.ShapeDtypeStruct((s, attn_w), BF16),
        compiler_params=_cp(("parallel",)),
        name="swa_attn",
    )(sink, qkv, qkv, qkv, qkv, qkv, qkv, qkv, bias, g)


def _t5_bucket(rel, n_buckets):
    half = n_buckets // 2
    max_exact = half // 2
    ret = jnp.where(rel > 0, half, 0)
    n = jnp.abs(rel)
    nf = jnp.maximum(n, 1).astype(F32)
    large = max_exact + (jnp.log(nf / max_exact) / math.log(MAX_DISTANCE / max_exact)
                         * (half - max_exact)).astype(I32)
    large = jnp.minimum(large, half - 1)
    return ret + jnp.where(n < max_exact, n, large)


def _bias_table(rel_bias):
    qi = jnp.arange(BLOCK)[:, None]
    kj = jnp.arange(3 * BLOCK)[None, :]
    rel = kj - BLOCK - qi
    return rel_bias.astype(F32)[_t5_bucket(rel, rel_bias.shape[0])].transpose(2, 0, 1)


def _lru_gates(xp_ref, xc_ref, xn_ref, cw_ref, cb_ref, wg_ref, bg_ref, sp_ref, a_ref, u_ref, *, first, last,
               n_blk):
    tr, w = xc_ref.shape
    bw = w // n_blk
    xp = jnp.where(first, 0.0, xp_ref[...])
    xn = jnp.where(last, 0.0, xn_ref[...])
    ext = jnp.concatenate([xp, xc_ref[...], xn], axis=0)
    n_ext = tr + 2 * SUBLANES
    cw = cw_ref[...]
    xc = (cw[0:1] * pltpu.roll(ext, 2, axis=0)[SUBLANES:SUBLANES + tr]
          + cw[1:2] * pltpu.roll(ext, 1, axis=0)[SUBLANES:SUBLANES + tr]
          + cw[2:3] * ext[SUBLANES:SUBLANES + tr]
          + cw[3:4] * pltpu.roll(ext, n_ext - 1, axis=0)[SUBLANES:SUBLANES + tr]) + cb_ref[...]
    xcb = xc.astype(BF16)
    for h in range(n_blk):
        sl = slice(h * bw, (h + 1) * bw)
        gi = jnp.dot(xcb[:, sl], wg_ref[h], preferred_element_type=F32) + bg_ref[h]
        r = jax.nn.sigmoid(gi[:, :bw])
        ig = jax.nn.sigmoid(gi[:, bw:])
        log_a = (-LRU_C) * r * sp_ref[:, sl]
        a_ref[:, sl] = jnp.exp(log_a)
        u_ref[:, sl] = jnp.sqrt(1.0 - jnp.exp(2.0 * log_a)) * (ig * xc[:, sl])


def _lru_fwd_kernel(xp_ref, xc_ref, xn_ref, cw_ref, cb_ref, wg_ref, bg_ref, sp_ref, o_ref,
                    a_ref, u_ref, carry_ref, *, nt, n_blk):
    i = pl.program_id(0)

    @pl.when(i == 0)
    def _():
        carry_ref[...] = jnp.zeros_like(carry_ref)

    _lru_gates(xp_ref, xc_ref, xn_ref, cw_ref, cb_ref, wg_ref, bg_ref, sp_ref, a_ref, u_ref,
               first=i == 0, last=i == nt - 1, n_blk=n_blk)
    tr = xc_ref.shape[0]

    def step(t, h):
        h = a_ref[pl.ds(t, 1), :] * h + u_ref[pl.ds(t, 1), :]
        o_ref[pl.ds(t, 1), :] = h
        return h

    carry_ref[...] = lax.fori_loop(0, tr, step, carry_ref[...], unroll=8)


def _lru_bwd_kernel(xp_ref, xc_ref, xn_ref, cw_ref, cb_ref, wg_ref, bg_ref, sp_ref, hf_ref, xg_ref, g_ref,
                    o_ref, a_ref, u_ref, hb_ref, carry_ref, *, nt, n_blk):
    i = pl.program_id(0)

    @pl.when(i == 0)
    def _():
        carry_ref[...] = jnp.zeros_like(carry_ref)

    _lru_gates(xp_ref, xc_ref, xn_ref, cw_ref, cb_ref, wg_ref, bg_ref, sp_ref, a_ref, u_ref,
               first=i == nt - 1, last=i == 0, n_blk=n_blk)
    tr = xc_ref.shape[0]

    def step(t, h):
        row = tr - 1 - t
        h = a_ref[pl.ds(row, 1), :] * h + u_ref[pl.ds(row, 1), :]
        hb_ref[pl.ds(row, 1), :] = h
        return h

    carry_ref[...] = lax.fori_loop(0, tr, step, carry_ref[...], unroll=8)
    lru = (hf_ref[...] + hb_ref[...]) * jax.nn.gelu(xg_ref[...])
    y = lru * lax.rsqrt(jnp.mean(jnp.square(lru), axis=-1, keepdims=True) + EPS)
    o_ref[...] = (y * g_ref[...]).astype(o_ref.dtype)


def _lru(xrg, cw, cb, wg, bg, sp, g, lru_w, tr):
    s = xrg.shape[0]
    nt = s // tr
    n_blk = wg.shape[1]
    hb8 = tr // SUBLANES
    n8 = s // SUBLANES

    def halo_specs(tile):
        return [pl.BlockSpec((SUBLANES, lru_w), lambda i: (jnp.maximum(tile(i) * hb8 - 1, 0), 0)),
                pl.BlockSpec((tr, lru_w), lambda i: (tile(i), 0)),
                pl.BlockSpec((SUBLANES, lru_w), lambda i: (jnp.minimum((tile(i) + 1) * hb8, n8 - 1), 0))]

    def param_specs(d):
        return [pl.BlockSpec((CONV_W, lru_w), lambda i: (0, 0)),
                pl.BlockSpec((1, lru_w), lambda i: (0, 0)),
                pl.BlockSpec((None, n_blk, lru_w // n_blk, 2 * lru_w // n_blk), lambda i: (d, 0, 0, 0)),
                pl.BlockSpec((None, n_blk, 1, 2 * lru_w // n_blk), lambda i: (d, 0, 0, 0)),
                pl.BlockSpec((None, 1, lru_w), lambda i: (d, 0, 0))]

    fwd_tile = lambda i: i
    hf = pl.pallas_call(
        functools.partial(_lru_fwd_kernel, nt=nt, n_blk=n_blk),
        grid=(nt,),
        in_specs=halo_specs(fwd_tile) + param_specs(0),
        out_specs=pl.BlockSpec((tr, lru_w), lambda i: (i, 0)),
        out_shape=jax.ShapeDtypeStruct((s, lru_w), F32),
        scratch_shapes=[pltpu.VMEM((tr, lru_w), F32), pltpu.VMEM((tr, lru_w), F32), pltpu.VMEM((1, lru_w), F32)],
        compiler_params=_cp(("arbitrary",)),
        name="lru_fwd",
    )(xrg, xrg, xrg, cw, cb, wg, bg, sp)

    bwd_tile = lambda i: nt - 1 - i
    return pl.pallas_call(
        functools.partial(_lru_bwd_kernel, nt=nt, n_blk=n_blk),
        grid=(nt,),
        in_specs=halo_specs(bwd_tile) + param_specs(1) + [
            pl.BlockSpec((tr, lru_w), lambda i: (nt - 1 - i, 0)),
            pl.BlockSpec((tr, lru_w), lambda i: (nt - 1 - i, 1)),
            pl.BlockSpec((1, lru_w), lambda i: (0, 0))],
        out_specs=pl.BlockSpec((tr, lru_w), lambda i: (nt - 1 - i, 0)),
        out_shape=jax.ShapeDtypeStruct((s, lru_w), BF16),
        scratch_shapes=[pltpu.VMEM((tr, lru_w), F32), pltpu.VMEM((tr, lru_w), F32), pltpu.VMEM((tr, lru_w), F32),
                        pltpu.VMEM((1, lru_w), F32)],
        compiler_params=_cp(("arbitrary",)),
        name="lru_bwd",
    )(xrg, xrg, xrg, cw, cb, wg, bg, sp, hf, xrg, g)


def _outproj_kernel(a1_ref, a2_ref, w1_ref, w2_ref, o_ref):
    o_ref[...] = (jnp.dot(a1_ref[...], w1_ref[...], preferred_element_type=F32)
                  + jnp.dot(a2_ref[...], w2_ref[...], preferred_element_type=F32))


def _outproj(a1, a2, w_bf, tm, tn):
    s, k1 = a1.shape
    k2 = a2.shape[1]
    assert k1 == k2
    n = w_bf.shape[1]
    return pl.pallas_call(
        _outproj_kernel,
        grid=(s // tm, n // tn),
        in_specs=[pl.BlockSpec((tm, k1), lambda i, j: (i, 0)),
                  pl.BlockSpec((tm, k2), lambda i, j: (i, 0)),
                  pl.BlockSpec((k1, tn), lambda i, j: (0, j)),
                  pl.BlockSpec((k2, tn), lambda i, j: (1, j))],
        out_specs=pl.BlockSpec((tm, tn), lambda i, j: (i, j)),
        out_shape=jax.ShapeDtypeStruct((s, n), F32),
        compiler_params=_cp(("parallel", "arbitrary")),
        name="out_proj",
    )(a1, a2, w_bf, w_bf)


def _pack_bf16_pair(hi, lo):
    hb = pltpu.bitcast(hi.astype(BF16).astype(F32), U32)
    lb = pltpu.bitcast(lo.astype(BF16).astype(F32), U32)
    return (hb & jnp.uint32(0xFFFF0000)) | (lb >> 16)


def _unpack_bf16_pair(p):
    hi = pltpu.bitcast(p & jnp.uint32(0xFFFF0000), F32).astype(BF16)
    lo = pltpu.bitcast(p << 16, F32).astype(BF16)
    return hi, lo


def _ln_router_kernel(x_ref, y_ref, g1_ref, lg_ref, lb_ref, sc_ref, sh_ref, rw_ref, rb_ref, cin_ref,
                      x1_ref, hp_ref, route_ref, cout_ref, cnt_ref, *, alpha):
    i = pl.program_id(0)

    @pl.when(i == 0)
    def _():
        cnt_ref[...] = cin_ref[...]

    z = alpha * x_ref[...] + g1_ref[...] * y_ref[...]
    mu = jnp.mean(z, axis=-1, keepdims=True)
    var = jnp.mean(jnp.square(z - mu), axis=-1, keepdims=True)
    x1 = (z - mu) * lax.rsqrt(var + EPS) * lg_ref[...] + lb_ref[...]
    x1_ref[...] = x1
    h = x1 * (1.0 + sc_ref[...]) + sh_ref[...]
    half = h.shape[1] // 2
    hp_ref[...] = _pack_bf16_pair(h[:, :half], h[:, half:])

    logits = jnp.dot(h, rw_ref[...], preferred_element_type=F32, precision=lax.Precision.HIGHEST) + rb_ref[...]
    tm = logits.shape[0]
    lane = lax.broadcasted_iota(I32, (tm, LANES), 1).astype(F32)
    ri = lax.broadcasted_iota(I32, (tm, tm), 0)
    ci = lax.broadcasted_iota(I32, (tm, tm), 1)
    ltri = jnp.where(ci < ri, 1.0, 0.0).astype(BF16)
    counts = cnt_ref[...]
    route = jnp.zeros((tm, LANES), F32)
    vals = []
    l = logits
    for k in range(TOP_K):
        m = jnp.max(l, axis=-1, keepdims=True)
        idx = jnp.min(jnp.where(l == m, lane, float(LANES)), axis=-1, keepdims=True)
        oh = lane == idx
        l = jnp.where(oh, NEG * 2.0, l)
        ohf = jnp.where(oh, 1.0, 0.0)
        before = jnp.dot(ltri, ohf.astype(BF16), preferred_element_type=F32) + counts
        rank = jnp.sum(ohf * before, axis=-1, keepdims=True)
        counts = counts + jnp.sum(ohf, axis=0, keepdims=True)
        vals.append(m)
        route = route + jnp.where(lane == float(k), idx, 0.0) + jnp.where(lane == float(2 * TOP_K + k), rank, 0.0)
    es = [jnp.exp(v - vals[0]) for v in vals]
    den = es[0] + es[1] + es[2] + es[3]
    for k in range(TOP_K):
        route = route + jnp.where(lane == float(TOP_K + k), es[k] / den, 0.0)
    route_ref[...] = route
    cnt_ref[...] = counts
    cout_ref[...] = counts


def _ln_router(x, y, g1, lg, lb, sc, sh, rw, rb, cin, tm):
    s, d = x.shape
    row = pl.BlockSpec((tm, d), lambda i: (i, 0))
    vec = pl.BlockSpec((1, d), lambda i: (0, 0))
    return pl.pallas_call(
        functools.partial(_ln_router_kernel, alpha=2.0 ** 0.25),
        grid=(s // tm,),
        in_specs=[row, row, vec, vec, vec, vec, vec,
                  pl.BlockSpec((d, LANES), lambda i: (0, 0)),
                  pl.BlockSpec((1, LANES), lambda i: (0, 0)),
                  pl.BlockSpec((1, LANES), lambda i: (0, 0))],
        out_specs=[row,
                   pl.BlockSpec((tm, d // 2), lambda i: (i, 0)),
                   pl.BlockSpec((tm, LANES), lambda i: (i, 0)),
                   pl.BlockSpec((1, LANES), lambda i: (0, 0))],
        out_shape=[jax.ShapeDtypeStruct((s, d), F32),
                   jax.ShapeDtypeStruct((s, d // 2), U32),
                   jax.ShapeDtypeStruct((s, LANES), F32),
                   jax.ShapeDtypeStruct((1, LANES), F32)],
        scratch_shapes=[pltpu.VMEM((1, LANES), F32)],
        compiler_params=_cp(("arbitrary",)),
        name="ln_router",
    )(x, y, g1, lg, lb, sc, sh, rw, rb, cin)


def _dispatch_kernel(pos_hbm, h_ref, xs_hbm, pos_smem, sem, psem, *, td):
    i = pl.program_id(0)
    nrow = td * TOP_K // LANES
    cp = pltpu.make_async_copy(pos_hbm.at[pl.ds(i * nrow, nrow)], pos_smem, psem)
    cp.start()
    cp.wait()

    def issue(g, c):
        for u in range(DMA_UNROLL):
            r = g * DMA_UNROLL + u
            p = pos_smem[r >> LANE_BITS, r & (LANES - 1)]
            pltpu.make_async_copy(h_ref.at[pl.ds(r >> TOPK_BITS, 1)], xs_hbm.at[pl.ds(p, 1)],
                                  sem).start(priority=u % 2)
        return c

    lax.fori_loop(0, td * TOP_K // DMA_UNROLL, issue, 0)
    for _ in range(TOP_K):
        pltpu.make_async_copy(h_ref, xs_hbm.at[pl.ds(0, td)], sem).wait()


def _dispatch(pos2d, hp, n_rows, td):
    t, dh = hp.shape
    return pl.pallas_call(
        functools.partial(_dispatch_kernel, td=td),
        grid=(t // td,),
        in_specs=[pl.BlockSpec(memory_space=pl.ANY),
                  pl.BlockSpec((td, dh), lambda i: (i, 0))],
        out_specs=pl.BlockSpec(memory_space=pl.ANY),
        out_shape=jax.ShapeDtypeStruct((n_rows, dh), U32),
        scratch_shapes=[pltpu.SMEM((td * TOP_K // LANES, LANES), I32),
                        pltpu.SemaphoreType.DMA, pltpu.SemaphoreType.DMA],
        compiler_params=_cp(("arbitrary",)),
        name="moe_dispatch",
    )(pos2d, hp)


def _moe_gu_kernel(ce_ref, cr_ref, nc_ref, x_ref, w_ref, b_ref, psel_ref, o_ref, xb_ref, h_ref, *, sb, nsb, nj):
    c = pl.program_id(0)
    j = pl.program_id(1)
    rows = cr_ref[c]
    half = x_ref.shape[1]
    tn = w_ref.shape[2]
    full = nsb * sb

    def product(r0, nr, slot):
        wb = w_ref[0].astype(BF16)
        h_ref[slot, r0:r0 + nr, :] = jnp.dot(xb_ref[r0:r0 + nr, :], wb, preferred_element_type=F32) + b_ref[0]

    def activation(r0, nr, slot):
        h = h_ref[slot, r0:r0 + nr, :]
        up = pltpu.roll(h, tn - 1, axis=1)
        gate = jnp.minimum(h, SWIGLU_LIMIT)
        up = jnp.clip(up, -SWIGLU_LIMIT, SWIGLU_LIMIT)
        act = gate * jax.nn.sigmoid(SWIGLU_ALPHA * gate) * (up + 1.0)
        even = (lax.broadcasted_iota(I32, (nr, tn), 1) & 1) == 0
        act = jnp.where(even, act, 0.0).astype(BF16)
        o_ref[r0:r0 + nr, :] = jnp.dot(act, psel_ref[...], preferred_element_type=F32).astype(o_ref.dtype)

    @pl.when(c < nc_ref[0])
    def _():
        @pl.when(j == 0)
        def _():
            hi, lo = _unpack_bf16_pair(x_ref[...])
            xb_ref[:, :half] = hi
            xb_ref[:, half:] = lo

        @pl.when((c == 0) & (j == 0))
        def _():
            h_ref[...] = jnp.zeros_like(h_ref)

        for slot in (0, 1):
            @pl.when((j < nj) & ((j & 1) == slot) & (rows == full))
            def _():
                product(0, full, slot)
                activation(0, full, 1 - slot)

            @pl.when((j < nj) & ((j & 1) == slot) & (rows < full))
            def _():
                for s in range(nsb):
                    @pl.when(s * sb < rows)
                    def _():
                        product(s * sb, sb, slot)
                        activation(s * sb, sb, 1 - slot)

        @pl.when(j == nj)
        def _():
            for s in range(nsb):
                @pl.when(s * sb < rows)
                def _():
                    activation(s * sb, sb, (nj - 1) & 1)


def _moe_gate_up(chunk_e, chunk_rows, n_chunks, xs, w_gu, b_gu, rc, sb, tn):
    n_rows, half = xs.shape
    n_exp, d, n2 = w_gu.shape
    nc_max = n_rows // rc
    nj = n2 // tn
    ceff = lambda c, nc: jnp.minimum(c, nc[0] - 1)
    jin = lambda c, j, nc: jnp.where(c < nc[0], jnp.minimum(j, nj - 1), nj - 1)
    jout = lambda c, j, nc: jnp.where(c < nc[0], jnp.maximum(j - 1, 0), nj - 1)
    psel = (jnp.arange(tn)[:, None] == 2 * jnp.arange(tn // 2)[None, :]).astype(BF16)
    grid_spec = pltpu.PrefetchScalarGridSpec(
        num_scalar_prefetch=3,
        grid=(nc_max, nj + 1),
        in_specs=[pl.BlockSpec((rc, half), lambda c, j, ce, cr, nc: (ceff(c, nc), 0)),
                  pl.BlockSpec((1, d, tn), lambda c, j, ce, cr, nc: (ce[ceff(c, nc)], 0, jin(c, j, nc))),
                  pl.BlockSpec((1, 1, tn), lambda c, j, ce, cr, nc: (ce[ceff(c, nc)], 0, jin(c, j, nc))),
                  pl.BlockSpec((tn, tn // 2), lambda c, j, ce, cr, nc: (0, 0))],
        out_specs=pl.BlockSpec((rc, tn // 2), lambda c, j, ce, cr, nc: (ceff(c, nc), jout(c, j, nc))),
        scratch_shapes=[pltpu.VMEM((rc, d), BF16), pltpu.VMEM((2, rc, tn), F32)],
    )
    return pl.pallas_call(
        functools.partial(_moe_gu_kernel, sb=sb, nsb=rc // sb, nj=nj),
        grid_spec=grid_spec,
        out_shape=jax.ShapeDtypeStruct((n_rows, n2 // 2), BF16),
        compiler_params=_cp(("arbitrary", "arbitrary")),
        name="moe_gate_up",
    )(chunk_e, chunk_rows, n_chunks, xs, w_gu, b_gu.reshape(n_exp, 1, n2), psel)


def _moe_down_kernel(ce_ref, cr_ref, nc_ref, a_ref, w_ref, b_ref, o_ref, *, sb, nsb):
    c = pl.program_id(0)
    rows = cr_ref[c]

    @pl.when(c < nc_ref[0])
    def _():
        def down_rows(r0, nr):
            wb = w_ref[0].astype(BF16)
            o_ref[r0:r0 + nr, :] = jnp.dot(a_ref[r0:r0 + nr, :], wb, preferred_element_type=F32) + b_ref[0]

        @pl.when(rows == nsb * sb)
        def _():
            down_rows(0, nsb * sb)

        @pl.when(rows < nsb * sb)
        def _():
            for s in range(nsb):
                @pl.when(s * sb < rows)
                def _():
                    down_rows(s * sb, sb)


def _moe_down(chunk_e, chunk_rows, n_chunks, act, w_d, b_d, rc, sb, tn):
    n_rows, f = act.shape
    n_exp, _, d = w_d.shape
    nc_max = n_rows // rc
    nj = d // tn
    ceff = lambda c, nc: jnp.minimum(c, nc[0] - 1)
    jeff = lambda c, j, nc: jnp.where(c < nc[0], j, nj - 1)
    grid_spec = pltpu.PrefetchScalarGridSpec(
        num_scalar_prefetch=3,
        grid=(nc_max, nj),
        in_specs=[pl.BlockSpec((rc, f), lambda c, j, ce, cr, nc: (ceff(c, nc), 0)),
                  pl.BlockSpec((1, f, tn), lambda c, j, ce, cr, nc: (ce[ceff(c, nc)], 0, jeff(c, j, nc))),
                  pl.BlockSpec((1, 1, tn), lambda c, j, ce, cr, nc: (ce[ceff(c, nc)], 0, jeff(c, j, nc)))],
        out_specs=pl.BlockSpec((rc, tn), lambda c, j, ce, cr, nc: (ceff(c, nc), jeff(c, j, nc))),
    )
    return pl.pallas_call(
        functools.partial(_moe_down_kernel, sb=sb, nsb=rc // sb),
        grid_spec=grid_spec,
        out_shape=jax.ShapeDtypeStruct((n_rows, d), F32),
        compiler_params=_cp(("arbitrary", "arbitrary")),
        name="moe_down",
    )(chunk_e, chunk_rows, n_chunks, act, w_d, b_d.reshape(n_exp, 1, d))


def _combine_kernel(pos_hbm, ys_hbm, x_ref, rt_ref, g2_ref, lg_ref, lb_ref, o_ref, pos_smem, ybuf, sem, psem,
                    *, tc, blk0, alpha):
    i = pl.program_id(0)
    nrow = tc * TOP_K // LANES
    cp = pltpu.make_async_copy(pos_hbm.at[pl.ds((blk0 + i) * nrow, nrow)], pos_smem, psem)
    cp.start()
    cp.wait()

    def issue(g, c):
        for u in range(DMA_UNROLL):
            r = g * DMA_UNROLL + u
            p = pos_smem[r >> LANE_BITS, r & (LANES - 1)]
            pltpu.make_async_copy(ys_hbm.at[pl.ds(p, 1)], ybuf.at[r & (TOP_K - 1), pl.ds(r >> TOPK_BITS, 1)],
                                  sem).start(priority=u % 2)
        return c

    lax.fori_loop(0, tc * TOP_K // DMA_UNROLL, issue, 0)
    pltpu.make_async_copy(ybuf, ybuf, sem).wait()

    rt = rt_ref[...]
    y = rt[:, TOP_K:TOP_K + 1] * ybuf[0]
    for k in range(1, TOP_K):
        y = y + rt[:, TOP_K + k:TOP_K + k + 1] * ybuf[k]
    z = alpha * x_ref[...] + g2_ref[...] * y
    mu = jnp.mean(z, axis=-1, keepdims=True)
    var = jnp.mean(jnp.square(z - mu), axis=-1, keepdims=True)
    o_ref[...] = (z - mu) * lax.rsqrt(var + EPS) * lg_ref[...] + lb_ref[...]


def _combine(pos2d, ys, x1, route, g2, lg, lb, tok0, tc):
    s, d = x1.shape
    row = pl.BlockSpec((tc, d), lambda i: (i, 0))
    vec = pl.BlockSpec((1, d), lambda i: (0, 0))
    return pl.pallas_call(
        functools.partial(_combine_kernel, tc=tc, blk0=tok0 // tc, alpha=2.0 ** 0.25),
        grid=(s // tc,),
        in_specs=[pl.BlockSpec(memory_space=pl.ANY), pl.BlockSpec(memory_space=pl.ANY),
                  row, pl.BlockSpec((tc, LANES), lambda i: (i, 0)), vec, vec, vec],
        out_specs=row,
        out_shape=jax.ShapeDtypeStruct((s, d), F32),
        scratch_shapes=[pltpu.SMEM((tc * TOP_K // LANES, LANES), I32),
                        pltpu.VMEM((TOP_K, tc, d), F32),
                        pltpu.SemaphoreType.DMA, pltpu.SemaphoreType.DMA],
        compiler_params=_cp(("arbitrary",)),
        name="moe_combine",
    )(pos2d, ys, x1, route, g2, lg, lb)


def _tiles(d, seqs):
    smin = min(seqs)
    t_all = sum(seqs)
    big = d >= 2048
    return dict(
        tm_in=_tile(smin, 512), tn_in=512 if big else 256,
        tr_lru=_tile(smin, 256),
        tm_out=_tile(smin, 512), tn_out=1024 if big else 256,
        tm_ln=_tile(smin, 256),
        td=_tile(smin, 256), tc=_tile(smin, 256),
        rc=1024 if t_all * TOP_K >= 16384 else 128,
        sb=256 if t_all * TOP_K >= 16384 else 64,
        tn_gu=512 if big else 256, tn_dn=512 if big else 256,
    )


def kernel(x_prompt, x_sample, c_prompt, c_sample, w_ada, b_ada, w_in, conv_w, conv_b, lru_w_r, lru_b_r, lru_w_i,
           lru_b_i, lru_lambda, attn_sink, rel_bias, attn_norm_g, lru_norm_g, w_out, ln1_g, ln1_b, router_w,
           router_b, w_gate_up, b_gate_up, w_down, b_down, ln2_g, ln2_b):
    assert w_ada.shape[0] == 1, "single-layer trunk"
    d = x_prompt.shape[-1]
    xs_in = [x_prompt.reshape(-1, d), x_sample.reshape(-1, d)]
    assert x_prompt.shape[0] == 1 and x_sample.shape[0] == 1
    seqs = [x.shape[0] for x in xs_in]
    attn_w = d // 2
    lru_w = d - attn_w
    proj_w = w_in.shape[-1]
    kv_w = (proj_w - attn_w - 2 * lru_w) // 2
    n_exp = router_w.shape[-1]
    n_blk = lru_w_r.shape[2]
    tl = _tiles(d, seqs)
    row = lambda v: v.reshape(1, -1).astype(F32)

    c2 = jnp.concatenate([c_prompt, c_sample], axis=0).T
    mod = _ada(c2, w_ada[0], row(b_ada[0]))

    w_in_bf = w_in[0].astype(BF16)
    w_out_bf = w_out[0].astype(BF16)
    wg = jnp.concatenate([lru_w_r[0], lru_w_i[0]], axis=-1).astype(BF16)
    bw = lru_w // n_blk
    bg = jnp.concatenate([lru_b_r[0].reshape(2, n_blk, 1, bw), lru_b_i[0].reshape(2, n_blk, 1, bw)], axis=-1)
    sp = jax.nn.softplus(-lru_lambda[0].astype(F32)).reshape(2, 1, lru_w)
    bias = _bias_table(rel_bias).reshape(kv_w // HEAD_DIM, -1, 3 * BLOCK)
    rw = jnp.zeros((d, LANES), F32).at[:, :n_exp].set(router_w[0])
    rb = jnp.full((1, LANES), NEG, F32).at[0, :n_exp].set(router_b[0])

    counts = jnp.zeros((1, LANES), F32)
    x1s, hps, routes = [], [], []
    for gi, x in enumerate(xs_in):
        m = [mod[gi:gi + 1, k * d:(k + 1) * d] for k in range(6)]
        sh1, sc1, g1, sh2, sc2, _ = m
        qkv = _inproj(x, sc1, sh1, w_in_bf, 0, attn_w + 2 * kv_w, BF16, tl["tm_in"], tl["tn_in"])
        xrg = _inproj(x, sc1, sh1, w_in_bf, attn_w + 2 * kv_w, 2 * lru_w, F32, tl["tm_in"], tl["tn_in"])
        attn = _attention(qkv, attn_sink[0].astype(F32), bias, row(attn_norm_g[0]), attn_w, kv_w)
        lru = _lru(xrg, conv_w[0].astype(F32), row(conv_b[0]), wg, bg, sp, row(lru_norm_g[0]), lru_w, tl["tr_lru"])
        y = _outproj(attn, lru, w_out_bf, tl["tm_out"], tl["tn_out"])
        x1, hp, route, counts = _ln_router(x, y, g1, row(ln1_g[0]), row(ln1_b[0]), sc2, sh2, rw, rb, counts,
                                           tl["tm_ln"])
        x1s.append(x1)
        hps.append(hp)
        routes.append(route)

    rc, sb = tl["rc"], tl["sb"]
    t_all = sum(seqs)
    route_all = jnp.concatenate(routes, axis=0)
    top_e = route_all[:, :TOP_K].astype(I32)
    rank = route_all[:, 2 * TOP_K:3 * TOP_K].astype(I32)
    cnt = counts[0, :n_exp].astype(I32)
    nchunk_e = (cnt + rc - 1) // rc
    chunk_end = jnp.cumsum(nchunk_e)
    chunk_start = chunk_end - nchunk_e
    pos = (chunk_start * rc)[top_e] + rank
    nc_max = t_all * TOP_K // rc + n_exp
    cidx = jnp.arange(nc_max, dtype=I32)
    chunk_e = jnp.minimum(jnp.searchsorted(chunk_end, cidx, side="right"), n_exp - 1).astype(I32)
    chunk_rows = jnp.clip(cnt[chunk_e] - (cidx - chunk_start[chunk_e]) * rc, 0, rc).astype(I32)
    n_chunks = chunk_end[-1:].astype(I32)
    pos2d = pos.reshape(-1, LANES).astype(I32)

    hp_all = jnp.concatenate(hps, axis=0)
    xs = _dispatch(pos2d, hp_all, nc_max * rc, tl["td"])
    act = _moe_gate_up(chunk_e, chunk_rows, n_chunks, xs, w_gate_up[0], b_gate_up[0], rc, sb, tl["tn_gu"])
    ys = _moe_down(chunk_e, chunk_rows, n_chunks, act, w_down[0], b_down[0], rc, sb, tl["tn_dn"])

    outs = []
    tok0 = 0
    for gi, x1 in enumerate(x1s):
        g2 = mod[gi:gi + 1, 5 * d:6 * d]
        out = _combine(pos2d, ys, x1, routes[gi], g2, row(ln2_g[0]), row(ln2_b[0]), tok0, tl["tc"])
        outs.append(out.reshape(1, seqs[gi], d))
        tok0 += seqs[gi]
    return tuple(outs)
```

```python
import functools
import math

import jax
import jax.numpy as jnp
from jax import lax
from jax.experimental import pallas as pl
from jax.experimental.pallas import tpu as pltpu

F32 = jnp.float32
BF16 = jnp.bfloat16
I32 = jnp.int32
U32 = jnp.uint32

HEAD_DIM = 128
BLOCK = 128
WINDOW = 128
MAX_DISTANCE = 128
CONV_W = 4
LRU_C = 8.0
TOP_K = 4
SWIGLU_LIMIT = 7.0
SWIGLU_ALPHA = 1.702
EPS = 1e-5
NEG = -1e30
LANES = 128
SUBLANES = 8
LANE_BITS = 7
TOPK_BITS = 2
assert 1 << LANE_BITS == LANES and 1 << TOPK_BITS == TOP_K
DMA_UNROLL = 8
W_SPLIT = 4
VMEM_LIMIT = 56 * 1024 * 1024


def _cp(sem, vmem=VMEM_LIMIT):
    return pltpu.CompilerParams(dimension_semantics=sem, vmem_limit_bytes=vmem)


def _tile(n, want):
    t = min(n, want)
    while n % t:
        t -= 8
    return t


def _ada_kernel(*refs, tk, nk):
    c_refs = refs[:W_SPLIT]
    w_refs = refs[W_SPLIT:2 * W_SPLIT]
    b_ref, o_ref, acc_ref = refs[2 * W_SPLIT:]
    k = pl.program_id(1)

    @pl.when(k == 0)
    def _():
        acc_ref[...] = jnp.zeros_like(acc_ref)

    tn = b_ref.shape[1]
    a0 = acc_ref[0]
    a1 = acc_ref[1]
    for c_ref, w_ref in zip(c_refs, w_refs):
        c = c_ref[...]
        cs = c * jax.nn.sigmoid(c)
        for r in range(tk // SUBLANES):
            w = w_ref[r * SUBLANES:(r + 1) * SUBLANES, :]
            c8 = cs[r * SUBLANES:(r + 1) * SUBLANES, :]
            a0 = a0 + w * jnp.broadcast_to(c8[:, 0:1], (SUBLANES, tn))
            a1 = a1 + w * jnp.broadcast_to(c8[:, 1:2], (SUBLANES, tn))
    acc_ref[0] = a0
    acc_ref[1] = a1

    @pl.when(k == nk - 1)
    def _():
        r0 = jnp.sum(acc_ref[0], axis=0, keepdims=True)
        r1 = jnp.sum(acc_ref[1], axis=0, keepdims=True)
        o_ref[...] = jnp.concatenate([r0, r1], axis=0) + b_ref[...]


def _ada(c2, w, b):
    d, n = w.shape
    tk = _tile(d // W_SPLIT, 256)
    tn = _tile(n, 1024)
    nk = d // (tk * W_SPLIT)
    return pl.pallas_call(
        functools.partial(_ada_kernel, tk=tk, nk=nk),
        grid=(n // tn, nk),
        in_specs=[pl.BlockSpec((tk, 2), functools.partial(lambda j, k, q: (k * W_SPLIT + q, 0), q=q))
                  for q in range(W_SPLIT)]
        + [pl.BlockSpec((tk, tn), functools.partial(lambda j, k, q: (k * W_SPLIT + q, j), q=q))
           for q in range(W_SPLIT)]
        + [pl.BlockSpec((1, tn), lambda j, k: (0, j))],
        out_specs=pl.BlockSpec((2, tn), lambda j, k: (0, j)),
        out_shape=jax.ShapeDtypeStruct((2, n), F32),
        scratch_shapes=[pltpu.VMEM((2, SUBLANES, tn), F32)],
        compiler_params=_cp(("parallel", "arbitrary")),
        name="ada_mod",
    )(*([c2] * W_SPLIT), *([w] * W_SPLIT), b)


def _inproj_kernel(x_ref, sc_ref, sh_ref, *refs):
    w_refs = refs[:W_SPLIT]
    o_ref, hb_ref = refs[W_SPLIT:]

    @pl.when(pl.program_id(1) == 0)
    def _():
        hb_ref[...] = (x_ref[...] * (1.0 + sc_ref[...]) + sh_ref[...]).astype(BF16)

    w = jnp.concatenate([w[...] for w in w_refs], axis=0)
    o_ref[...] = jnp.dot(hb_ref[...], w, preferred_element_type=F32).astype(o_ref.dtype)


def _inproj(x, sc, sh, w_bf, col0, ncols, out_dtype, tm, tn):
    s, d = x.shape
    off = col0 // tn
    return pl.pallas_call(
        _inproj_kernel,
        grid=(s // tm, ncols // tn),
        in_specs=[pl.BlockSpec((tm, d), lambda i, j: (i, 0)),
                  pl.BlockSpec((1, d), lambda i, j: (0, 0)),
                  pl.BlockSpec((1, d), lambda i, j: (0, 0))]
        + [pl.BlockSpec((d // W_SPLIT, tn), functools.partial(lambda i, j, q: (q, j + off), q=q))
           for q in range(W_SPLIT)],
        out_specs=pl.BlockSpec((tm, tn), lambda i, j: (i, j)),
        out_shape=jax.ShapeDtypeStruct((s, ncols), out_dtype),
        scratch_shapes=[pltpu.VMEM((tm, d), BF16)],
        compiler_params=_cp(("parallel", "arbitrary")),
        name="in_proj",
    )(x, sc, sh, *([w_bf] * W_SPLIT))


def _attn_kernel(sink_ref, q_ref, kp_ref, kc_ref, kn_ref, vp_ref, vc_ref, vn_ref, bias_ref, g_ref, o_ref,
                 *, seq, n_kv, group, scale):
    i = pl.program_id(0)
    kw = jnp.concatenate([kp_ref[...], kc_ref[...], kn_ref[...]], axis=0)
    vw = jnp.concatenate([vp_ref[...], vc_ref[...], vn_ref[...]], axis=0)
    qi = lax.broadcasted_iota(I32, (BLOCK, 3 * BLOCK), 0)
    kj = lax.broadcasted_iota(I32, (BLOCK, 3 * BLOCK), 1)
    rel = kj - BLOCK - qi
    kpos = (i - 1) * BLOCK + kj
    mask = (jnp.abs(rel) <= WINDOW) & (kpos >= 0) & (kpos < seq)
    mask = jnp.concatenate([mask] * group, axis=0)
    outs = []
    for h in range(n_kv):
        kh = kw[:, h * HEAD_DIM:(h + 1) * HEAD_DIM]
        vh = vw[:, h * HEAD_DIM:(h + 1) * HEAD_DIM]
        heads = range(h * group, (h + 1) * group)
        qh = jnp.concatenate([q_ref[:, hq * HEAD_DIM:(hq + 1) * HEAD_DIM] for hq in heads], axis=0)
        sk = jnp.concatenate([jnp.full((BLOCK, 1), sink_ref[hq], F32) for hq in heads], axis=0)
        s = lax.dot_general(qh, kh, (((1,), (1,)), ((), ())), preferred_element_type=F32) * scale
        s = jnp.where(mask, s + bias_ref[h], NEG)
        m = jnp.maximum(jnp.max(s, axis=-1, keepdims=True), sk)
        p = jnp.exp(s - m)
        p = p / (jnp.sum(p, axis=-1, keepdims=True) + jnp.exp(sk - m))
        oh = jnp.dot(p.astype(BF16), vh, preferred_element_type=F32)
        outs += [oh[g * BLOCK:(g + 1) * BLOCK] for g in range(group)]
    o = jnp.concatenate(outs, axis=-1)
    y = o * lax.rsqrt(jnp.mean(jnp.square(o), axis=-1, keepdims=True) + EPS)
    o_ref[...] = (y * g_ref[...]).astype(o_ref.dtype)


def _attention(qkv, sink, bias, g, attn_w, kv_w):
    s = qkv.shape[0]
    nb = s // BLOCK
    n_kv = kv_w // HEAD_DIM
    n_q = attn_w // HEAD_DIM
    kcol = attn_w // kv_w
    vcol = kcol + 1
    prev = lambda i: jnp.maximum(i - 1, 0)
    nxt = lambda i: jnp.minimum(i + 1, nb - 1)
    kern = functools.partial(_attn_kernel, seq=s, n_kv=n_kv, group=n_q // n_kv, scale=HEAD_DIM ** -0.5)
    return pl.pallas_call(
        kern,
        grid=(nb,),
        in_specs=[pl.BlockSpec(memory_space=pltpu.SMEM),
                  pl.BlockSpec((BLOCK, attn_w), lambda i: (i, 0)),
                  pl.BlockSpec((BLOCK, kv_w), lambda i: (prev(i), kcol)),
                  pl.BlockSpec((BLOCK, kv_w), lambda i: (i, kcol)),
                  pl.BlockSpec((BLOCK, kv_w), lambda i: (nxt(i), kcol)),
                  pl.BlockSpec((BLOCK, kv_w), lambda i: (prev(i), vcol)),
                  pl.BlockSpec((BLOCK, kv_w), lambda i: (i, vcol)),
                  pl.BlockSpec((BLOCK, kv_w), lambda i: (nxt(i), vcol)),
                  pl.BlockSpec((n_kv, (n_q // n_kv) * BLOCK, 3 * BLOCK), lambda i: (0, 0, 0)),
                  pl.BlockSpec((1, attn_w), lambda i: (0, 0))],
        out_specs=pl.BlockSpec((BLOCK, attn_w), lambda i: (i, 0)),
        out_shape=jax.ShapeDtypeStruct((s, attn_w), BF16),
        compiler_params=_cp(("parallel",)),
        name="swa_attn",
    )(sink, qkv, qkv, qkv, qkv, qkv, qkv, qkv, bias, g)


def _t5_bucket(rel, n_buckets):
    half = n_buckets // 2
    max_exact = half // 2
    ret = jnp.where(rel > 0, half, 0)
    n = jnp.abs(rel)
    nf = jnp.maximum(n, 1).astype(F32)
    large = max_exact + (jnp.log(nf / max_exact) / math.log(MAX_DISTANCE / max_exact)
                         * (half - max_exact)).astype(I32)
    large = jnp.minimum(large, half - 1)
    return ret + jnp.where(n < max_exact, n, large)


def _bias_table(rel_bias):
    qi = jnp.arange(BLOCK)[:, None]
    kj = jnp.arange(3 * BLOCK)[None, :]
    rel = kj - BLOCK - qi
    return rel_bias.astype(F32)[_t5_bucket(rel, rel_bias.shape[0])].transpose(2, 0, 1)


def _lru_gates(xp_ref, xc_ref, xn_ref, cw_ref, cb_ref, wg_ref, bg_ref, sp_ref, a_ref, u_ref, *, first, last,
               n_blk):
    tr, w = xc_ref.shape
    bw = w // n_blk
    xp = jnp.where(first, 0.0, xp_ref[...])
    xn = jnp.where(last, 0.0, xn_ref[...])
    ext = jnp.concatenate([xp, xc_ref[...], xn], axis=0)
    n_ext = tr + 2 * SUBLANES
    cw = cw_ref[...]
    xc = (cw[0:1] * pltpu.roll(ext, 2, axis=0)[SUBLANES:SUBLANES + tr]
          + cw[1:2] * pltpu.roll(ext, 1, axis=0)[SUBLANES:SUBLANES + tr]
          + cw[2:3] * ext[SUBLANES:SUBLANES + tr]
          + cw[3:4] * pltpu.roll(ext, n_ext - 1, axis=0)[SUBLANES:SUBLANES + tr]) + cb_ref[...]
    xcb = xc.astype(BF16)
    for h in range(n_blk):
        sl = slice(h * bw, (h + 1) * bw)
        gi = jnp.dot(xcb[:, sl], wg_ref[h], preferred_element_type=F32) + bg_ref[h]
        r = jax.nn.sigmoid(gi[:, :bw])
        ig = jax.nn.sigmoid(gi[:, bw:])
        log_a = (-LRU_C) * r * sp_ref[:, sl]
        a_ref[:, sl] = jnp.exp(log_a)
        u_ref[:, sl] = jnp.sqrt(1.0 - jnp.exp(2.0 * log_a)) * (ig * xc[:, sl])


def _lru_fwd_kernel(xp_ref, xc_ref, xn_ref, cw_ref, cb_ref, wg_ref, bg_ref, sp_ref, o_ref,
                    a_ref, u_ref, carry_ref, *, nt, n_blk):
    i = pl.program_id(0)

    @pl.when(i == 0)
    def _():
        carry_ref[...] = jnp.zeros_like(carry_ref)

    _lru_gates(xp_ref, xc_ref, xn_ref, cw_ref, cb_ref, wg_ref, bg_ref, sp_ref, a_ref, u_ref,
               first=i == 0, last=i == nt - 1, n_blk=n_blk)
    tr = xc_ref.shape[0]

    def step(t, h):
        h = a_ref[pl.ds(t, 1), :] * h + u_ref[pl.ds(t, 1), :]
        o_ref[pl.ds(t, 1), :] = h
        return h

    carry_ref[...] = lax.fori_loop(0, tr, step, carry_ref[...], unroll=8)


def _lru_bwd_kernel(xp_ref, xc_ref, xn_ref, cw_ref, cb_ref, wg_ref, bg_ref, sp_ref, hf_ref, xg_ref, g_ref,
                    o_ref, a_ref, u_ref, hb_ref, carry_ref, *, nt, n_blk):
    i = pl.program_id(0)

    @pl.when(i == 0)
    def _():
        carry_ref[...] = jnp.zeros_like(carry_ref)

    _lru_gates(xp_ref, xc_ref, xn_ref, cw_ref, cb_ref, wg_ref, bg_ref, sp_ref, a_ref, u_ref,
               first=i == nt - 1, last=i == 0, n_blk=n_blk)
    tr = xc_ref.shape[0]

    def step(t, h):
        row = tr - 1 - t
        h = a_ref[pl.ds(row, 1), :] * h + u_ref[pl.ds(row, 1), :]
        hb_ref[pl.ds(row, 1), :] = h
        return h

    carry_ref[...] = lax.fori_loop(0, tr, step, carry_ref[...], unroll=8)
    lru = (hf_ref[...] + hb_ref[...]) * jax.nn.gelu(xg_ref[...])
    y = lru * lax.rsqrt(jnp.mean(jnp.square(lru), axis=-1, keepdims=True) + EPS)
    o_ref[...] = (y * g_ref[...]).astype(o_ref.dtype)


def _lru(xrg, cw, cb, wg, bg, sp, g, lru_w, tr):
    s = xrg.shape[0]
    nt = s // tr
    n_blk = wg.shape[1]
    hb8 = tr // SUBLANES
    n8 = s // SUBLANES

    def halo_specs(tile):
        return [pl.BlockSpec((SUBLANES, lru_w), lambda i: (jnp.maximum(tile(i) * hb8 - 1, 0), 0)),
                pl.BlockSpec((tr, lru_w), lambda i: (tile(i), 0)),
                pl.BlockSpec((SUBLANES, lru_w), lambda i: (jnp.minimum((tile(i) + 1) * hb8, n8 - 1), 0))]

    def param_specs(d):
        return [pl.BlockSpec((CONV_W, lru_w), lambda i: (0, 0)),
                pl.BlockSpec((1, lru_w), lambda i: (0, 0)),
                pl.BlockSpec((None, n_blk, lru_w // n_blk, 2 * lru_w // n_blk), lambda i: (d, 0, 0, 0)),
                pl.BlockSpec((None, n_blk, 1, 2 * lru_w // n_blk), lambda i: (d, 0, 0, 0)),
                pl.BlockSpec((None, 1, lru_w), lambda i: (d, 0, 0))]

    fwd_tile = lambda i: i
    hf = pl.pallas_call(
        functools.partial(_lru_fwd_kernel, nt=nt, n_blk=n_blk),
        grid=(nt,),
        in_specs=halo_specs(fwd_tile) + param_specs(0),
        out_specs=pl.BlockSpec((tr, lru_w), lambda i: (i, 0)),
        out_shape=jax.ShapeDtypeStruct((s, lru_w), F32),
        scratch_shapes=[pltpu.VMEM((tr, lru_w), F32), pltpu.VMEM((tr, lru_w), F32), pltpu.VMEM((1, lru_w), F32)],
        compiler_params=_cp(("arbitrary",)),
        name="lru_fwd",
    )(xrg, xrg, xrg, cw, cb, wg, bg, sp)

    bwd_tile = lambda i: nt - 1 - i
    return pl.pallas_call(
        functools.partial(_lru_bwd_kernel, nt=nt, n_blk=n_blk),
        grid=(nt,),
        in_specs=halo_specs(bwd_tile) + param_specs(1) + [
            pl.BlockSpec((tr, lru_w), lambda i: (nt - 1 - i, 0)),
            pl.BlockSpec((tr, lru_w), lambda i: (nt - 1 - i, 1)),
            pl.BlockSpec((1, lru_w), lambda i: (0, 0))],
        out_specs=pl.BlockSpec((tr, lru_w), lambda i: (nt - 1 - i, 0)),
        out_shape=jax.ShapeDtypeStruct((s, lru_w), BF16),
        scratch_shapes=[pltpu.VMEM((tr, lru_w), F32), pltpu.VMEM((tr, lru_w), F32), pltpu.VMEM((tr, lru_w), F32),
                        pltpu.VMEM((1, lru_w), F32)],
        compiler_params=_cp(("arbitrary",)),
        name="lru_bwd",
    )(xrg, xrg, xrg, cw, cb, wg, bg, sp, hf, xrg, g)


def _outproj_kernel(a1_ref, a2_ref, w1_ref, w2_ref, o_ref):
    o_ref[...] = (jnp.dot(a1_ref[...], w1_ref[...], preferred_element_type=F32)
                  + jnp.dot(a2_ref[...], w2_ref[...], preferred_element_type=F32))


def _outproj(a1, a2, w_bf, tm, tn):
    s, k1 = a1.shape
    k2 = a2.shape[1]
    assert k1 == k2
    n = w_bf.shape[1]
    return pl.pallas_call(
        _outproj_kernel,
        grid=(s // tm, n // tn),
        in_specs=[pl.BlockSpec((tm, k1), lambda i, j: (i, 0)),
                  pl.BlockSpec((tm, k2), lambda i, j: (i, 0)),
                  pl.BlockSpec((k1, tn), lambda i, j: (0, j)),
                  pl.BlockSpec((k2, tn), lambda i, j: (1, j))],
        out_specs=pl.BlockSpec((tm, tn), lambda i, j: (i, j)),
        out_shape=jax.ShapeDtypeStruct((s, n), F32),
        compiler_params=_cp(("parallel", "arbitrary")),
        name="out_proj",
    )(a1, a2, w_bf, w_bf)


def _pack_bf16_pair(hi, lo):
    hb = pltpu.bitcast(hi.astype(BF16).astype(F32), U32)
    lb = pltpu.bitcast(lo.astype(BF16).astype(F32), U32)
    return (hb & jnp.uint32(0xFFFF0000)) | (lb >> 16)


def _unpack_bf16_pair(p):
    hi = pltpu.bitcast(p & jnp.uint32(0xFFFF0000), F32).astype(BF16)
    lo = pltpu.bitcast(p << 16, F32).astype(BF16)
    return hi, lo


def _ln_router_kernel(x_ref, y_ref, g1_ref, lg_ref, lb_ref, sc_ref, sh_ref, rw_ref, rb_ref, cin_ref,
                      x1_ref, hp_ref, route_ref, cout_ref, cnt_ref, *, alpha):
    i = pl.program_id(0)

    @pl.when(i == 0)
    def _():
        cnt_ref[...] = cin_ref[...]

    z = alpha * x_ref[...] + g1_ref[...] * y_ref[...]
    mu = jnp.mean(z, axis=-1, keepdims=True)
    var = jnp.mean(jnp.square(z - mu), axis=-1, keepdims=True)
    x1 = (z - mu) * lax.rsqrt(var + EPS) * lg_ref[...] + lb_ref[...]
    x1_ref[...] = x1
    h = x1 * (1.0 + sc_ref[...]) + sh_ref[...]
    half = h.shape[1] // 2
    hp_ref[...] = _pack_bf16_pair(h[:, :half], h[:, half:])

    logits = jnp.dot(h, rw_ref[...], preferred_element_type=F32, precision=lax.Precision.HIGHEST) + rb_ref[...]
    tm = logits.shape[0]
    lane = lax.broadcasted_iota(I32, (tm, LANES), 1).astype(F32)
    ri = lax.broadcasted_iota(I32, (tm, tm), 0)
    ci = lax.broadcasted_iota(I32, (tm, tm), 1)
    ltri = jnp.where(ci < ri, 1.0, 0.0).astype(BF16)
    counts = cnt_ref[...]
    route = jnp.zeros((tm, LANES), F32)
    vals = []
    l = logits
    for k in range(TOP_K):
        m = jnp.max(l, axis=-1, keepdims=True)
        idx = jnp.min(jnp.where(l == m, lane, float(LANES)), axis=-1, keepdims=True)
        oh = lane == idx
        l = jnp.where(oh, NEG * 2.0, l)
        ohf = jnp.where(oh, 1.0, 0.0)
        before = jnp.dot(ltri, ohf.astype(BF16), preferred_element_type=F32) + counts
        rank = jnp.sum(ohf * before, axis=-1, keepdims=True)
        counts = counts + jnp.sum(ohf, axis=0, keepdims=True)
        vals.append(m)
        route = route + jnp.where(lane == float(k), idx, 0.0) + jnp.where(lane == float(2 * TOP_K + k), rank, 0.0)
    es = [jnp.exp(v - vals[0]) for v in vals]
    den = es[0] + es[1] + es[2] + es[3]
    for k in range(TOP_K):
        route = route + jnp.where(lane == float(TOP_K + k), es[k] / den, 0.0)
    route_ref[...] = route
    cnt_ref[...] = counts
    cout_ref[...] = counts


def _ln_router_alias_kernel(x_ref, y_ref, g1_ref, lg_ref, lb_ref, sc_ref, sh_ref, rw_ref, rb_ref, cin_ref,
                            hp_all_ref, x1_ref, hp_ref, route_ref, cout_ref, cnt_ref, *, alpha):
    del hp_all_ref
    _ln_router_kernel(x_ref, y_ref, g1_ref, lg_ref, lb_ref, sc_ref, sh_ref, rw_ref, rb_ref, cin_ref,
                      x1_ref, hp_ref, route_ref, cout_ref, cnt_ref, alpha=alpha)


def _ln_router(x, y, g1, lg, lb, sc, sh, rw, rb, cin, hp_all, tok0, t_all, tm):
    s, d = x.shape
    blk0 = tok0 // tm
    row = pl.BlockSpec((tm, d), lambda i: (i, 0))
    vec = pl.BlockSpec((1, d), lambda i: (0, 0))
    in_specs = [row, row, vec, vec, vec, vec, vec,
                pl.BlockSpec((d, LANES), lambda i: (0, 0)),
                pl.BlockSpec((1, LANES), lambda i: (0, 0)),
                pl.BlockSpec((1, LANES), lambda i: (0, 0))]
    args = [x, y, g1, lg, lb, sc, sh, rw, rb, cin]
    kern = functools.partial(_ln_router_kernel, alpha=2.0 ** 0.25)
    aliases = {}
    if hp_all is not None:
        in_specs.append(pl.BlockSpec(memory_space=pl.ANY))
        args.append(hp_all)
        aliases = {len(args) - 1: 1}
        kern = functools.partial(_ln_router_alias_kernel, alpha=2.0 ** 0.25)
    return pl.pallas_call(
        kern,
        grid=(s // tm,),
        in_specs=in_specs,
        out_specs=[row,
                   pl.BlockSpec((tm, d // 2), lambda i: (i + blk0, 0)),
                   pl.BlockSpec((tm, LANES), lambda i: (i, 0)),
                   pl.BlockSpec((1, LANES), lambda i: (0, 0))],
        out_shape=[jax.ShapeDtypeStruct((s, d), F32),
                   jax.ShapeDtypeStruct((t_all, d // 2), U32),
                   jax.ShapeDtypeStruct((s, LANES), F32),
                   jax.ShapeDtypeStruct((1, LANES), F32)],
        scratch_shapes=[pltpu.VMEM((1, LANES), F32)],
        input_output_aliases=aliases,
        compiler_params=_cp(("arbitrary",)),
        name="ln_router",
    )(*args)


def _dispatch_kernel(pos_hbm, h_ref, xs_hbm, pos_smem, sem, psem, *, td):
    i = pl.program_id(0)
    nrow = td * TOP_K // LANES
    cp = pltpu.make_async_copy(pos_hbm.at[pl.ds(i * nrow, nrow)], pos_smem, psem)
    cp.start()
    cp.wait()

    def issue(g, c):
        for u in range(DMA_UNROLL):
            r = g * DMA_UNROLL + u
            p = pos_smem[r >> LANE_BITS, r & (LANES - 1)]
            pltpu.make_async_copy(h_ref.at[pl.ds(r >> TOPK_BITS, 1)], xs_hbm.at[pl.ds(p, 1)],
                                  sem).start(priority=u % 2)
        return c

    lax.fori_loop(0, td * TOP_K // DMA_UNROLL, issue, 0)
    for _ in range(TOP_K):
        pltpu.make_async_copy(h_ref, xs_hbm.at[pl.ds(0, td)], sem).wait()


def _dispatch(pos2d, hp, n_rows, td):
    t, dh = hp.shape
    return pl.pallas_call(
        functools.partial(_dispatch_kernel, td=td),
        grid=(t // td,),
        in_specs=[pl.BlockSpec(memory_space=pl.ANY),
                  pl.BlockSpec((td, dh), lambda i: (i, 0))],
        out_specs=pl.BlockSpec(memory_space=pl.ANY),
        out_shape=jax.ShapeDtypeStruct((n_rows, dh), U32),
        scratch_shapes=[pltpu.SMEM((td * TOP_K // LANES, LANES), I32),
                        pltpu.SemaphoreType.DMA, pltpu.SemaphoreType.DMA],
        compiler_params=_cp(("arbitrary",)),
        name="moe_dispatch",
    )(pos2d, hp)


def _cast_weight_tile(w_refs):
    return jnp.concatenate([w[0].astype(BF16) for w in w_refs], axis=0)


def _moe_gu_kernel(ce_ref, cr_ref, nc_ref, x_ref, *refs, sb, nsb, nj):
    w_refs = refs[:W_SPLIT]
    b_ref, psel_ref, o_ref, xb_ref, h_ref = refs[W_SPLIT:]
    c = pl.program_id(0)
    j = pl.program_id(1)
    rows = cr_ref[c]
    half = x_ref.shape[1]
    tn = b_ref.shape[2]
    full = nsb * sb

    def product(r0, nr, slot):
        wb = _cast_weight_tile(w_refs)
        h_ref[slot, r0:r0 + nr, :] = jnp.dot(xb_ref[r0:r0 + nr, :], wb, preferred_element_type=F32) + b_ref[0]

    def activation(r0, nr, slot):
        h = h_ref[slot, r0:r0 + nr, :]
        up = pltpu.roll(h, tn - 1, axis=1)
        gate = jnp.minimum(h, SWIGLU_LIMIT)
        up = jnp.clip(up, -SWIGLU_LIMIT, SWIGLU_LIMIT)
        act = gate * jax.nn.sigmoid(SWIGLU_ALPHA * gate) * (up + 1.0)
        even = (lax.broadcasted_iota(I32, (nr, tn), 1) & 1) == 0
        act = jnp.where(even, act, 0.0).astype(BF16)
        o_ref[r0:r0 + nr, :] = jnp.dot(act, psel_ref[...], preferred_element_type=F32).astype(o_ref.dtype)

    @pl.when(c < nc_ref[0])
    def _():
        @pl.when(j == 0)
        def _():
            hi, lo = _unpack_bf16_pair(x_ref[...])
            xb_ref[:, :half] = hi
            xb_ref[:, half:] = lo

        @pl.when((c == 0) & (j == 0))
        def _():
            h_ref[...] = jnp.zeros_like(h_ref)

        for slot in (0, 1):
            @pl.when((j < nj) & ((j & 1) == slot) & (rows == full))
            def _():
                product(0, full, slot)
                activation(0, full, 1 - slot)

            @pl.when((j < nj) & ((j & 1) == slot) & (rows < full))
            def _():
                for s in range(nsb):
                    @pl.when(s * sb < rows)
                    def _():
                        product(s * sb, sb, slot)
                        activation(s * sb, sb, 1 - slot)

        @pl.when(j == nj)
        def _():
            for s in range(nsb):
                @pl.when(s * sb < rows)
                def _():
                    activation(s * sb, sb, (nj - 1) & 1)


def _moe_gate_up(chunk_e, chunk_rows, n_chunks, xs, w_gu, b_gu, rc, sb, tn):
    n_rows, half = xs.shape
    n_exp, d, n2 = w_gu.shape
    nc_max = n_rows // rc
    nj = n2 // tn
    ceff = lambda c, nc: jnp.minimum(c, nc[0] - 1)
    jin = lambda c, j, nc: jnp.where(c < nc[0], jnp.minimum(j, nj - 1), nj - 1)
    jout = lambda c, j, nc: jnp.where(c < nc[0], jnp.maximum(j - 1, 0), nj - 1)
    psel = (jnp.arange(tn)[:, None] == 2 * jnp.arange(tn // 2)[None, :]).astype(BF16)
    grid_spec = pltpu.PrefetchScalarGridSpec(
        num_scalar_prefetch=3,
        grid=(nc_max, nj + 1),
        in_specs=[pl.BlockSpec((rc, half), lambda c, j, ce, cr, nc: (ceff(c, nc), 0))]
        + [pl.BlockSpec((1, d // W_SPLIT, tn), functools.partial(
            lambda c, j, ce, cr, nc, q: (ce[ceff(c, nc)], q, jin(c, j, nc)), q=q)) for q in range(W_SPLIT)]
        + [pl.BlockSpec((1, 1, tn), lambda c, j, ce, cr, nc: (ce[ceff(c, nc)], 0, jin(c, j, nc))),
           pl.BlockSpec((tn, tn // 2), lambda c, j, ce, cr, nc: (0, 0))],
        out_specs=pl.BlockSpec((rc, tn // 2), lambda c, j, ce, cr, nc: (ceff(c, nc), jout(c, j, nc))),
        scratch_shapes=[pltpu.VMEM((rc, d), BF16), pltpu.VMEM((2, rc, tn), F32)],
    )
    return pl.pallas_call(
        functools.partial(_moe_gu_kernel, sb=sb, nsb=rc // sb, nj=nj),
        grid_spec=grid_spec,
        out_shape=jax.ShapeDtypeStruct((n_rows, n2 // 2), BF16),
        compiler_params=_cp(("arbitrary", "arbitrary")),
        name="moe_gate_up",
    )(chunk_e, chunk_rows, n_chunks, xs, *([w_gu] * W_SPLIT), b_gu.reshape(n_exp, 1, n2), psel)


def _moe_down_kernel(ce_ref, cr_ref, nc_ref, a_ref, *refs, sb, nsb):
    w_refs = refs[:W_SPLIT]
    b_ref, o_ref = refs[W_SPLIT:]
    c = pl.program_id(0)
    rows = cr_ref[c]

    @pl.when(c < nc_ref[0])
    def _():
        def down_rows(r0, nr):
            wb = _cast_weight_tile(w_refs)
            o_ref[r0:r0 + nr, :] = jnp.dot(a_ref[r0:r0 + nr, :], wb, preferred_element_type=F32) + b_ref[0]

        @pl.when(rows == nsb * sb)
        def _():
            down_rows(0, nsb * sb)

        @pl.when(rows < nsb * sb)
        def _():
            for s in range(nsb):
                @pl.when(s * sb < rows)
                def _():
                    down_rows(s * sb, sb)


def _moe_down(chunk_e, chunk_rows, n_chunks, act, w_d, b_d, rc, sb, tn):
    n_rows, f = act.shape
    n_exp, _, d = w_d.shape
    nc_max = n_rows // rc
    nj = d // tn
    ceff = lambda c, nc: jnp.minimum(c, nc[0] - 1)
    jeff = lambda c, j, nc: jnp.where(c < nc[0], j, nj - 1)
    grid_spec = pltpu.PrefetchScalarGridSpec(
        num_scalar_prefetch=3,
        grid=(nc_max, nj),
        in_specs=[pl.BlockSpec((rc, f), lambda c, j, ce, cr, nc: (ceff(c, nc), 0))]
        + [pl.BlockSpec((1, f // W_SPLIT, tn), functools.partial(
            lambda c, j, ce, cr, nc, q: (ce[ceff(c, nc)], q, jeff(c, j, nc)), q=q)) for q in range(W_SPLIT)]
        + [pl.BlockSpec((1, 1, tn), lambda c, j, ce, cr, nc: (ce[ceff(c, nc)], 0, jeff(c, j, nc)))],
        out_specs=pl.BlockSpec((rc, tn), lambda c, j, ce, cr, nc: (ceff(c, nc), jeff(c, j, nc))),
    )
    return pl.pallas_call(
        functools.partial(_moe_down_kernel, sb=sb, nsb=rc // sb),
        grid_spec=grid_spec,
        out_shape=jax.ShapeDtypeStruct((n_rows, d), F32),
        compiler_params=_cp(("arbitrary", "arbitrary")),
        name="moe_down",
    )(chunk_e, chunk_rows, n_chunks, act, *([w_d] * W_SPLIT), b_d.reshape(n_exp, 1, d))


def _combine_kernel(pos_hbm, ys_hbm, x_ref, rt_ref, g2_ref, lg_ref, lb_ref, o_ref, pos_smem, ybuf, sem, psem,
                    *, tc, blk0, alpha):
    i = pl.program_id(0)
    nrow = tc * TOP_K // LANES
    cp = pltpu.make_async_copy(pos_hbm.at[pl.ds((blk0 + i) * nrow, nrow)], pos_smem, psem)
    cp.start()
    cp.wait()

    def issue(g, c):
        for u in range(DMA_UNROLL):
            r = g * DMA_UNROLL + u
            p = pos_smem[r >> LANE_BITS, r & (LANES - 1)]
            pltpu.make_async_copy(ys_hbm.at[pl.ds(p, 1)], ybuf.at[r & (TOP_K - 1), pl.ds(r >> TOPK_BITS, 1)],
                                  sem).start(priority=u % 2)
        return c

    lax.fori_loop(0, tc * TOP_K // DMA_UNROLL, issue, 0)
    pltpu.make_async_copy(ybuf, ybuf, sem).wait()

    rt = rt_ref[...]
    y = rt[:, TOP_K:TOP_K + 1] * ybuf[0]
    for k in range(1, TOP_K):
        y = y + rt[:, TOP_K + k:TOP_K + k + 1] * ybuf[k]
    z = alpha * x_ref[...] + g2_ref[...] * y
    mu = jnp.mean(z, axis=-1, keepdims=True)
    var = jnp.mean(jnp.square(z - mu), axis=-1, keepdims=True)
    o_ref[...] = (z - mu) * lax.rsqrt(var + EPS) * lg_ref[...] + lb_ref[...]


def _combine(pos2d, ys, x1, route, g2, lg, lb, tok0, tc):
    s, d = x1.shape
    row = pl.BlockSpec((tc, d), lambda i: (i, 0))
    vec = pl.BlockSpec((1, d), lambda i: (0, 0))
    return pl.pallas_call(
        functools.partial(_combine_kernel, tc=tc, blk0=tok0 // tc, alpha=2.0 ** 0.25),
        grid=(s // tc,),
        in_specs=[pl.BlockSpec(memory_space=pl.ANY), pl.BlockSpec(memory_space=pl.ANY),
                  row, pl.BlockSpec((tc, LANES), lambda i: (i, 0)), vec, vec, vec],
        out_specs=row,
        out_shape=jax.ShapeDtypeStruct((s, d), F32),
        scratch_shapes=[pltpu.SMEM((tc * TOP_K // LANES, LANES), I32),
                        pltpu.VMEM((TOP_K, tc, d), F32),
                        pltpu.SemaphoreType.DMA, pltpu.SemaphoreType.DMA],
        compiler_params=_cp(("arbitrary",)),
        name="moe_combine",
    )(pos2d, ys, x1, route, g2, lg, lb)


def _tiles(d, seqs):
    smin = min(seqs)
    t_all = sum(seqs)
    big = d >= 2048
    return dict(
        tm_in=_tile(smin, 512), tn_in=512 if big else 256,
        tr_lru=_tile(smin, 256),
        tm_out=_tile(smin, 512), tn_out=1024 if big else 256,
        tm_ln=_tile(smin, 256),
        td=_tile(smin, 256), tc=_tile(smin, 256),
        rc=1024 if t_all * TOP_K >= 16384 else 128,
        sb=256 if t_all * TOP_K >= 16384 else 64,
        tn_gu=512 if big else 256, tn_dn=512 if big else 256,
    )


def kernel(x_prompt, x_sample, c_prompt, c_sample, w_ada, b_ada, w_in, conv_w, conv_b, lru_w_r, lru_b_r, lru_w_i,
           lru_b_i, lru_lambda, attn_sink, rel_bias, attn_norm_g, lru_norm_g, w_out, ln1_g, ln1_b, router_w,
           router_b, w_gate_up, b_gate_up, w_down, b_down, ln2_g, ln2_b):
    assert w_ada.shape[0] == 1, "single-layer trunk"
    d = x_prompt.shape[-1]
    xs_in = [x_prompt.reshape(-1, d), x_sample.reshape(-1, d)]
    assert x_prompt.shape[0] == 1 and x_sample.shape[0] == 1
    seqs = [x.shape[0] for x in xs_in]
    attn_w = d // 2
    lru_w = d - attn_w
    proj_w = w_in.shape[-1]
    kv_w = (proj_w - attn_w - 2 * lru_w) // 2
    n_exp = router_w.shape[-1]
    n_blk = lru_w_r.shape[2]
    tl = _tiles(d, seqs)
    row = lambda v: v.reshape(1, -1).astype(F32)

    c2 = jnp.concatenate([c_prompt, c_sample], axis=0).T
    mod = _ada(c2, w_ada[0], row(b_ada[0]))

    w_in_bf = w_in[0].astype(BF16)
    w_out_bf = w_out[0].astype(BF16)
    wg = jnp.concatenate([lru_w_r[0], lru_w_i[0]], axis=-1).astype(BF16)
    bw = lru_w // n_blk
    bg = jnp.concatenate([lru_b_r[0].reshape(2, n_blk, 1, bw), lru_b_i[0].reshape(2, n_blk, 1, bw)], axis=-1)
    sp = jax.nn.softplus(-lru_lambda[0].astype(F32)).reshape(2, 1, lru_w)
    bias = _bias_table(rel_bias).reshape(kv_w // HEAD_DIM, -1, 3 * BLOCK)
    rw = jnp.zeros((d, LANES), F32).at[:, :n_exp].set(router_w[0])
    rb = jnp.full((1, LANES), NEG, F32).at[0, :n_exp].set(router_b[0])

    counts = jnp.zeros((1, LANES), F32)
    x1s, routes = [], []
    hp_all = None
    t_all = sum(seqs)
    tok0 = 0
    for gi, x in enumerate(xs_in):
        m = [mod[gi:gi + 1, k * d:(k + 1) * d] for k in range(6)]
        sh1, sc1, g1, sh2, sc2, _ = m
        qkv = _inproj(x, sc1, sh1, w_in_bf, 0, attn_w + 2 * kv_w, BF16, tl["tm_in"], tl["tn_in"])
        xrg = _inproj(x, sc1, sh1, w_in_bf, attn_w + 2 * kv_w, 2 * lru_w, F32, tl["tm_in"], tl["tn_in"])
        attn = _attention(qkv, attn_sink[0].astype(F32), bias, row(attn_norm_g[0]), attn_w, kv_w)
        lru = _lru(xrg, conv_w[0].astype(F32), row(conv_b[0]), wg, bg, sp, row(lru_norm_g[0]), lru_w, tl["tr_lru"])
        y = _outproj(attn, lru, w_out_bf, tl["tm_out"], tl["tn_out"])
        x1, hp_all, route, counts = _ln_router(x, y, g1, row(ln1_g[0]), row(ln1_b[0]), sc2, sh2, rw, rb, counts,
                                               hp_all, tok0, t_all, tl["tm_ln"])
        tok0 += seqs[gi]
        x1s.append(x1)
        routes.append(route)

    rc, sb = tl["rc"], tl["sb"]
    route_all = jnp.concatenate(routes, axis=0)
    top_e = route_all[:, :TOP_K].astype(I32)
    rank = route_all[:, 2 * TOP_K:3 * TOP_K].astype(I32)
    cnt = counts[0, :n_exp].astype(I32)
    nchunk_e = (cnt + rc - 1) // rc
    chunk_end = jnp.cumsum(nchunk_e)
    chunk_start = chunk_end - nchunk_e
    seg_of = jnp.sum(jnp.where(top_e[..., None] == jnp.arange(n_exp, dtype=I32), chunk_start * rc, 0), axis=-1)
    pos = seg_of + rank
    nc_max = t_all * TOP_K // rc + n_exp
    cidx = jnp.arange(nc_max, dtype=I32)
    chunk_e = jnp.minimum(jnp.searchsorted(chunk_end, cidx, side="right"), n_exp - 1).astype(I32)
    chunk_rows = jnp.clip(cnt[chunk_e] - (cidx - chunk_start[chunk_e]) * rc, 0, rc).astype(I32)
    n_chunks = chunk_end[-1:].astype(I32)
    pos2d = pos.reshape(-1, LANES).astype(I32)

    xs = _dispatch(pos2d, hp_all, nc_max * rc, tl["td"])
    act = _moe_gate_up(chunk_e, chunk_rows, n_chunks, xs, w_gate_up[0], b_gate_up[0], rc, sb, tl["tn_gu"])
    ys = _moe_down(chunk_e, chunk_rows, n_chunks, act, w_down[0], b_down[0], rc, sb, tl["tn_dn"])

    outs = []
    tok0 = 0
    for gi, x1 in enumerate(x1s):
        g2 = mod[gi:gi + 1, 5 * d:6 * d]
        out = _combine(pos2d, ys, x1, routes[gi], g2, row(ln2_g[0]), row(ln2_b[0]), tok0, tl["tc"])
        outs.append(out.reshape(1, seqs[gi], d))
        tok0 += seqs[gi]
    return tuple(outs)
```
